```python
import math
import jax, jax.numpy as jnp
from jax import lax
import numpy as np

D_MODEL = 1024
BATCH = 8
SEQ = 4096
DEPTH = 4
DEC_BATCH = 8
DEC_SEQ = 16
PAST_LEN = 2048

CHUNK = 64
CONV_K = 4
EPS = 1e-6
MIXER_CYCLE = ('lru', 'mlstm', 'ssd')
LAYER_KINDS = tuple(MIXER_CYCLE[i % len(MIXER_CYCLE)] for i in range(DEPTH))
LAYER_SLOT = tuple(LAYER_KINDS[:i].count(LAYER_KINDS[i]) for i in range(DEPTH))
N_LRU = LAYER_KINDS.count('lru')
N_MLSTM = LAYER_KINDS.count('mlstm')
N_SSD = LAYER_KINDS.count('ssd')

D_RNN = 2 * D_MODEL
LRU_BLOCKS = 16
LRU_BS = D_RNN // LRU_BLOCKS
LRU_C = 8.0
ML_INNER = 2 * D_MODEL
ML_HEADS = 8
ML_DH = ML_INNER // ML_HEADS
ML_QKV_BS = 4
SSD_INNER = 2 * D_MODEL
SSD_HEADDIM = 64
SSD_HEADS = SSD_INNER // SSD_HEADDIM
SSD_GROUPS = 4
SSD_HPG = SSD_HEADS // SSD_GROUPS
SSD_DSTATE = 128
SSD_CONV_DIM = SSD_INNER + 2 * SSD_GROUPS * SSD_DSTATE
SSD_PROJ = SSD_INNER + SSD_CONV_DIM + SSD_HEADS

kernel_name = 'hybrid_lru_mlstm_ssd_stream_step'

F32 = jnp.float32


def rmsnorm(x, g):
    xf = x.astype(F32)
    y = xf * lax.rsqrt(jnp.mean(xf * xf, axis=-1, keepdims=True) + EPS)
    return (y * g.astype(F32)).astype(x.dtype)


def causal_conv(x, buf, w, b):
    T = x.shape[1]
    xp = jnp.concatenate([buf.astype(x.dtype), x], axis=1)
    y = sum(xp[:, k:k + T] * w[k] for k in range(CONV_K)) + b
    return y, xp[:, -(CONV_K - 1):]


def block_diag(x, w):
    B, T, _ = x.shape
    nb, bs, _ = w.shape
    return jnp.einsum('btni,nij->btnj', x.reshape(B, T, nb, bs), w).reshape(B, T, nb * bs)


def chunk_len(T):
    return CHUNK if T % CHUNK == 0 else T


def to_chunks(a, L):
    B, T = a.shape[:2]
    return jnp.swapaxes(a.reshape((B, T // L, L) + a.shape[2:]), 0, 1)


def from_chunks(a):
    a = jnp.swapaxes(a, 0, 1)
    return a.reshape((a.shape[0], a.shape[1] * a.shape[2]) + a.shape[3:])


def linear_recurrence(a, b, h0):
    b = b.at[:, 0].add(a[:, 0] * h0)
    def combine(l, r):
        return (l[0] * r[0], r[0] * l[1] + r[1])
    _, h = lax.associative_scan(combine, (a, b), axis=1)
    return h


def lru_mixer(x, conv_buf, h0, w_in, conv_w, conv_b, w_a, b_a, w_x, b_x, lam, w_out):
    u, g = jnp.split(x.astype(F32) @ w_in, 2, axis=-1)
    u, new_buf = causal_conv(u, conv_buf, conv_w, conv_b)
    r = jax.nn.sigmoid(block_diag(u, w_a) + b_a)
    i = jax.nn.sigmoid(block_diag(u, w_x) + b_x)
    log_a = -LRU_C * r * jax.nn.softplus(-lam.astype(F32))
    a = jnp.exp(log_a)
    bterm = jnp.sqrt(-jnp.expm1(2.0 * log_a)) * (i * u)
    h = linear_recurrence(a, bterm, h0.astype(F32))
    y = (h * jax.nn.silu(g)) @ w_out
    return y, new_buf, h[:, -1]


def mlstm_chunkwise(q, k, v, ig, lf, C0, n0, m0):
    L = chunk_len(q.shape[1])
    causal = jnp.tril(jnp.ones((L, L), bool))

    def step(carry, inp):
        C, n, m = carry
        qc, kc, vc, igc, lfc = inp
        b = jnp.swapaxes(jnp.cumsum(lfc, axis=1), 1, 2)
        igh = jnp.swapaxes(igc, 1, 2)
        logD = jnp.where(causal, b[..., :, None] - b[..., None, :] + igh[..., None, :], -jnp.inf)
        log_inter = b + m[..., None]
        mt = jnp.maximum(log_inter, jnp.max(logD, axis=-1))
        S = jnp.einsum('bthd,bshd->bhts', qc, kc) * jnp.exp(logD - mt[..., None])
        w_int = jnp.exp(log_inter - mt)
        num = jnp.einsum('bhts,bshd->bhtd', S, vc) + w_int[..., None] * jnp.einsum('bthk,bhkv->bhtv', qc, C)
        den = jnp.sum(S, axis=-1) + w_int * jnp.einsum('bthk,bhk->bht', qc, n)
        hc = num / jnp.maximum(jnp.abs(den), jnp.exp(-mt))[..., None]
        m_new = mt[..., -1]
        w_state = jnp.exp(b[..., -1:] - b + igh - m_new[..., None])
        decay = jnp.exp(b[..., -1] + m - m_new)
        C_new = decay[..., None, None] * C + jnp.einsum('bhs,bshk,bshv->bhkv', w_state, kc, vc)
        n_new = decay[..., None] * n + jnp.einsum('bhs,bshk->bhk', w_state, kc)
        return (C_new, n_new, m_new), jnp.swapaxes(hc, 1, 2)

    xs = tuple(to_chunks(t, L) for t in (q, k, v, ig, lf))
    (C, n, m), h = lax.scan(step, (C0.astype(F32), n0.astype(F32), m0.astype(F32)), xs)
    return from_chunks(h), C, n, m


def mlstm_mixer(x, conv_buf, C0, n0, m0, w_in, conv_w, conv_b, w_q, w_k, w_v, w_ig, b_ig,
                w_fg, b_fg, ln_w, skip, w_out):
    B, T, _ = x.shape
    u, z = jnp.split(x.astype(F32) @ w_in, 2, axis=-1)
    uc, new_buf = causal_conv(u, conv_buf, conv_w, conv_b)
    uc = jax.nn.silu(uc)
    q = block_diag(uc, w_q)
    k = block_diag(uc, w_k) * (ML_DH ** -0.5)
    v = block_diag(u, w_v)
    qkv = jnp.concatenate([q, k, v], axis=-1)
    ig = qkv @ w_ig + b_ig
    lf = jax.nn.log_sigmoid(qkv @ w_fg + b_fg)
    heads = lambda t: t.reshape(B, T, ML_HEADS, ML_DH)
    h, C, n, m = mlstm_chunkwise(heads(q), heads(k), heads(v), ig, lf, C0, n0, m0)
    mu = jnp.mean(h, axis=-1, keepdims=True)
    var = jnp.mean(jnp.square(h - mu), axis=-1, keepdims=True)
    h = ((h - mu) * lax.rsqrt(var + EPS)).reshape(B, T, ML_INNER) * ln_w
    y = ((h + skip * uc) * jax.nn.silu(z)) @ w_out
    return y, new_buf, C, n, m


def ssd_chunkwise(xs, dt, A, Bm, Cm, S0):
    L = chunk_len(xs.shape[1])
    causal = jnp.tril(jnp.ones((L, L), bool))

    def step(S, inp):
        xc, dtc, Bc, Cc = inp
        cs = jnp.cumsum(dtc * A, axis=1)
        csh = jnp.moveaxis(cs, 1, -1)
        decay = jnp.exp(jnp.where(causal, csh[..., :, None] - csh[..., None, :], -jnp.inf))
        CB = jnp.einsum('btgn,bsgn->bgts', Cc, Bc)
        y = jnp.einsum('bgts,bghts,bsghp->btghp', CB, decay, xc * dtc[..., None])
        y = y + jnp.exp(cs)[..., None] * jnp.einsum('btgn,bghpn->btghp', Cc, S)
        w = jnp.exp(cs[:, -1:] - cs) * dtc
        S_new = jnp.exp(cs[:, -1])[..., None, None] * S + jnp.einsum('bsgn,bsgh,bsghp->bghpn', Bc, w, xc)
        return S_new, y

    S, y = lax.scan(step, S0, tuple(to_chunks(t, L) for t in (xs, dt, Bm, Cm)))
    return from_chunks(y), S


def ssd_mixer(x, conv_buf, S0, w_in, conv_w, conv_b, dt_bias, a_log, d_skip, norm_w, w_out):
    B, T, _ = x.shape
    G, N, P = SSD_GROUPS, SSD_DSTATE, SSD_HEADDIM
    proj = x.astype(F32) @ w_in
    z = proj[..., :SSD_INNER]
    xbc = proj[..., SSD_INNER:SSD_INNER + SSD_CONV_DIM]
    dt = proj[..., SSD_INNER + SSD_CONV_DIM:]
    xbc, new_buf = causal_conv(xbc, conv_buf, conv_w, conv_b)
    xbc = jax.nn.silu(xbc)
    xs = xbc[..., :SSD_INNER].reshape(B, T, G, SSD_HPG, P)
    Bm = xbc[..., SSD_INNER:SSD_INNER + G * N].reshape(B, T, G, N)
    Cm = xbc[..., SSD_INNER + G * N:].reshape(B, T, G, N)
    dt = jax.nn.softplus(dt + dt_bias.astype(F32)).reshape(B, T, G, SSD_HPG)
    A = -jnp.exp(a_log.astype(F32)).reshape(G, SSD_HPG)
    y, S = ssd_chunkwise(xs, dt, A, Bm, Cm, S0.astype(F32).reshape(B, G, SSD_HPG, P, N))
    y = y + d_skip.astype(F32).reshape(G, SSD_HPG)[..., None] * xs
    y = y.reshape(B, T, SSD_INNER) * jax.nn.silu(z)
    yg = y.reshape(B, T, G, SSD_INNER // G)
    yg = yg * lax.rsqrt(jnp.mean(yg * yg, axis=-1, keepdims=True) + EPS)
    y = yg.reshape(B, T, SSD_INNER) * norm_w
    return y @ w_out, new_buf, S.reshape(B, SSD_HEADS, P, N)


def zero_state(kind, b):
    z = lambda *s: jnp.zeros((b,) + s, F32)
    if kind == 'lru':
        return [z(CONV_K - 1, D_RNN), z(D_RNN)]
    if kind == 'mlstm':
        return [z(CONV_K - 1, ML_INNER), z(ML_HEADS, ML_DH, ML_DH), z(ML_HEADS, ML_DH), z(ML_HEADS)]
    return [z(CONV_K - 1, SSD_CONV_DIM), z(SSD_HEADS, SSD_HEADDIM, SSD_DSTATE)]


def setup_inputs(seed: int = 0) -> dict:
    key = jax.random.key(seed)
    ks = iter(jax.random.split(key, 64))
    nrm = lambda shape, scale: scale * jax.random.normal(next(ks), shape, F32)
    gain = lambda shape: 1.0 + nrm(shape, 0.02)
    x_prompt = nrm((BATCH, SEQ, D_MODEL), 1.0)
    x_sample = nrm((DEC_BATCH, DEC_SEQ, D_MODEL), 1.0)
    state_l0_conv = nrm((DEC_BATCH, CONV_K - 1, D_RNN), 1.0)
    state_l0_h = nrm((DEC_BATCH, D_RNN), 0.5)
    state_l1_conv = nrm((DEC_BATCH, CONV_K - 1, ML_INNER), 1.0)
    state_l1_C = nrm((DEC_BATCH, ML_HEADS, ML_DH, ML_DH), ML_DH ** -0.5)
    state_l1_n = nrm((DEC_BATCH, ML_HEADS, ML_DH), ML_DH ** -0.5)
    state_l1_m = nrm((DEC_BATCH, ML_HEADS), 0.5)
    state_l2_conv = nrm((DEC_BATCH, CONV_K - 1, SSD_CONV_DIM), 1.0)
    state_l2_ssm = nrm((DEC_BATCH, SSD_HEADS, SSD_HEADDIM, SSD_DSTATE), 0.1)
    state_l3_conv = nrm((DEC_BATCH, CONV_K - 1, D_RNN), 1.0)
    state_l3_h = nrm((DEC_BATCH, D_RNN), 0.5)
    norm_pre = gain((DEPTH, D_MODEL))
    norm_post = gain((DEPTH, D_MODEL))
    lru_w_in = nrm((N_LRU, D_MODEL, 2 * D_RNN), D_MODEL ** -0.5)
    lru_conv_w = nrm((N_LRU, CONV_K, D_RNN), CONV_K ** -0.5)
    lru_conv_b = nrm((N_LRU, D_RNN), 0.01)
    lru_w_a = nrm((N_LRU, LRU_BLOCKS, LRU_BS, LRU_BS), LRU_BS ** -0.5)
    lru_b_a = nrm((N_LRU, D_RNN), 0.01)
    lru_w_x = nrm((N_LRU, LRU_BLOCKS, LRU_BS, LRU_BS), LRU_BS ** -0.5)
    lru_b_x = nrm((N_LRU, D_RNN), 0.01)
    a0 = jax.random.uniform(next(ks), (N_LRU, D_RNN), F32, 0.9, 0.999)
    s = a0 ** (1.0 / LRU_C)
    lru_lambda = jnp.log(s) - jnp.log1p(-s)
    lru_w_out = nrm((N_LRU, D_RNN, D_MODEL), D_RNN ** -0.5)
    mlstm_w_in = nrm((N_MLSTM, D_MODEL, 2 * ML_INNER), D_MODEL ** -0.5)
    mlstm_conv_w = nrm((N_MLSTM, CONV_K, ML_INNER), CONV_K ** -0.5)
    mlstm_conv_b = nrm((N_MLSTM, ML_INNER), 0.01)
    nb = ML_INNER // ML_QKV_BS
    mlstm_w_q = nrm((N_MLSTM, nb, ML_QKV_BS, ML_QKV_BS), ML_QKV_BS ** -0.5)
    mlstm_w_k = nrm((N_MLSTM, nb, ML_QKV_BS, ML_QKV_BS), ML_QKV_BS ** -0.5)
    mlstm_w_v = nrm((N_MLSTM, nb, ML_QKV_BS, ML_QKV_BS), ML_QKV_BS ** -0.5)
    mlstm_w_ig = nrm((N_MLSTM, 3 * ML_INNER, ML_HEADS), (3 * ML_INNER) ** -0.5)
    mlstm_b_ig = nrm((N_MLSTM, ML_HEADS), 0.1)
    mlstm_w_fg = nrm((N_MLSTM, 3 * ML_INNER, ML_HEADS), (3 * ML_INNER) ** -0.5)
    mlstm_b_fg = jnp.linspace(3.0, 6.0, ML_HEADS, dtype=F32)[None] + nrm((N_MLSTM, ML_HEADS), 0.1)
    mlstm_ln_w = gain((N_MLSTM, ML_INNER))
    mlstm_skip = gain((N_MLSTM, ML_INNER))
    mlstm_w_out = nrm((N_MLSTM, ML_INNER, D_MODEL), ML_INNER ** -0.5)
    ssd_w_in = nrm((N_SSD, D_MODEL, SSD_PROJ), D_MODEL ** -0.5)
    ssd_conv_w = nrm((N_SSD, CONV_K, SSD_CONV_DIM), CONV_K ** -0.5)
    ssd_conv_b = nrm((N_SSD, SSD_CONV_DIM), 0.01)
    dt0 = jnp.exp(jax.random.uniform(next(ks), (N_SSD, SSD_HEADS), F32, math.log(1e-3), math.log(1e-1)))
    ssd_dt_bias = dt0 + jnp.log(-jnp.expm1(-dt0))
    ssd_a_log = jnp.log(jax.random.uniform(next(ks), (N_SSD, SSD_HEADS), F32, 1.0, 16.0))
    ssd_d_skip = 1.0 + nrm((N_SSD, SSD_HEADS), 0.1)
    ssd_norm_w = gain((N_SSD, SSD_INNER))
    ssd_w_out = nrm((N_SSD, SSD_INNER, D_MODEL), SSD_INNER ** -0.5)
    return {'x_prompt': x_prompt, 'x_sample': x_sample,
            'state_l0_conv': state_l0_conv, 'state_l0_h': state_l0_h,
            'state_l1_conv': state_l1_conv, 'state_l1_C': state_l1_C, 'state_l1_n': state_l1_n, 'state_l1_m': state_l1_m,
            'state_l2_conv': state_l2_conv, 'state_l2_ssm': state_l2_ssm,
            'state_l3_conv': state_l3_conv, 'state_l3_h': state_l3_h,
            'norm_pre': norm_pre, 'norm_post': norm_post,
            'lru_w_in': lru_w_in, 'lru_conv_w': lru_conv_w, 'lru_conv_b': lru_conv_b, 'lru_w_a': lru_w_a,
            'lru_b_a': lru_b_a, 'lru_w_x': lru_w_x, 'lru_b_x': lru_b_x, 'lru_lambda': lru_lambda, 'lru_w_out': lru_w_out,
            'mlstm_w_in': mlstm_w_in, 'mlstm_conv_w': mlstm_conv_w, 'mlstm_conv_b': mlstm_conv_b,
            'mlstm_w_q': mlstm_w_q, 'mlstm_w_k': mlstm_w_k, 'mlstm_w_v': mlstm_w_v,
            'mlstm_w_ig': mlstm_w_ig, 'mlstm_b_ig': mlstm_b_ig, 'mlstm_w_fg': mlstm_w_fg, 'mlstm_b_fg': mlstm_b_fg,
            'mlstm_ln_w': mlstm_ln_w, 'mlstm_skip': mlstm_skip, 'mlstm_w_out': mlstm_w_out,
            'ssd_w_in': ssd_w_in, 'ssd_conv_w': ssd_conv_w, 'ssd_conv_b': ssd_conv_b, 'ssd_dt_bias': ssd_dt_bias,
            'ssd_a_log': ssd_a_log, 'ssd_d_skip': ssd_d_skip, 'ssd_norm_w': ssd_norm_w, 'ssd_w_out': ssd_w_out}


def reference(x_prompt, x_sample,
              state_l0_conv, state_l0_h,
              state_l1_conv, state_l1_C, state_l1_n, state_l1_m,
              state_l2_conv, state_l2_ssm,
              state_l3_conv, state_l3_h,
              norm_pre, norm_post,
              lru_w_in, lru_conv_w, lru_conv_b, lru_w_a, lru_b_a, lru_w_x, lru_b_x, lru_lambda, lru_w_out,
              mlstm_w_in, mlstm_conv_w, mlstm_conv_b, mlstm_w_q, mlstm_w_k, mlstm_w_v,
              mlstm_w_ig, mlstm_b_ig, mlstm_w_fg, mlstm_b_fg, mlstm_ln_w, mlstm_skip, mlstm_w_out,
              ssd_w_in, ssd_conv_w, ssd_conv_b, ssd_dt_bias, ssd_a_log, ssd_d_skip, ssd_norm_w, ssd_w_out):
    lru_p = (lru_w_in, lru_conv_w, lru_conv_b, lru_w_a, lru_b_a, lru_w_x, lru_b_x, lru_lambda, lru_w_out)
    ml_p = (mlstm_w_in, mlstm_conv_w, mlstm_conv_b, mlstm_w_q, mlstm_w_k, mlstm_w_v, mlstm_w_ig, mlstm_b_ig,
            mlstm_w_fg, mlstm_b_fg, mlstm_ln_w, mlstm_skip, mlstm_w_out)
    ssd_p = (ssd_w_in, ssd_conv_w, ssd_conv_b, ssd_dt_bias, ssd_a_log, ssd_d_skip, ssd_norm_w, ssd_w_out)

    def trunk(x, states):
        new_states = []
        for i in range(DEPTH):
            kind, j = LAYER_KINDS[i], LAYER_SLOT[i]
            h = rmsnorm(x, norm_pre[i])
            if kind == 'lru':
                y, *st = lru_mixer(h, *states[i], *(p[j] for p in lru_p))
            elif kind == 'mlstm':
                y, *st = mlstm_mixer(h, *states[i], *(p[j] for p in ml_p))
            else:
                y, *st = ssd_mixer(h, *states[i], *(p[j] for p in ssd_p))
            x = x + rmsnorm(y.astype(x.dtype), norm_post[i])
            new_states.append([s.astype(x.dtype) for s in st])
        return x, new_states

    prompt_init = [zero_state(LAYER_KINDS[i], x_prompt.shape[0]) for i in range(DEPTH)]
    sample_init = [[state_l0_conv, state_l0_h],
                   [state_l1_conv, state_l1_C, state_l1_n, state_l1_m],
                   [state_l2_conv, state_l2_ssm],
                   [state_l3_conv, state_l3_h]]
    y_prompt, p_states = trunk(x_prompt, prompt_init)
    y_sample, s_states = trunk(x_sample, sample_init)
    (p_l0_conv, p_l0_h), (p_l1_conv, p_l1_C, p_l1_n, p_l1_m), (p_l2_conv, p_l2_ssm), (p_l3_conv, p_l3_h) = p_states
    (s_l0_conv, s_l0_h), (s_l1_conv, s_l1_C, s_l1_n, s_l1_m), (s_l2_conv, s_l2_ssm), (s_l3_conv, s_l3_h) = s_states
    return (y_prompt, y_sample,
            p_l0_conv, p_l0_h, p_l1_conv, p_l1_C, p_l1_n, p_l1_m, p_l2_conv, p_l2_ssm, p_l3_conv, p_l3_h,
            s_l0_conv, s_l0_h, s_l1_conv, s_l1_C, s_l1_n, s_l1_m, s_l2_conv, s_l2_ssm, s_l3_conv, s_l3_h)
```

```python
import functools

import jax
import jax.numpy as jnp
from jax import lax
from jax.experimental import pallas as pl
from jax.experimental.pallas import tpu as pltpu

F32 = jnp.float32
BF16 = jnp.bfloat16
EPS = 1e-6
CONV_K = 4
LRU_C = 8.0
LANES = 128
SUBLANES = 8
CONV_BASE = SUBLANES
VMEM_LIMIT_BYTES = 58 * 1024 * 1024
NEG_INF = float("-inf")


def _mm(a, b):
    return jnp.dot(a.astype(BF16), b.astype(BF16), preferred_element_type=F32)


def _mm_nt(a, b):
    return lax.dot_general(a.astype(BF16), b.astype(BF16), (((1,), (1,)), ((), ())),
                           preferred_element_type=F32)


def _mm_tn(a, b):
    return lax.dot_general(a.astype(BF16), b.astype(BF16), (((0,), (0,)), ((), ())),
                           preferred_element_type=F32)


def _mm_f32(a, b):
    return jnp.dot(a, b, precision=lax.Precision.HIGHEST, preferred_element_type=F32)


def _transpose_f32(a):
    n = a.shape[1]
    eye = (lax.broadcasted_iota(jnp.int32, (n, n), 0) ==
           lax.broadcasted_iota(jnp.int32, (n, n), 1)).astype(F32)
    return lax.dot_general(eye, a, (((1,), (1,)), ((), ())),
                           precision=lax.Precision.HIGHEST, preferred_element_type=F32)


def _rmsnorm(x, g):
    return x * lax.rsqrt(jnp.mean(x * x, axis=-1, keepdims=True) + EPS) * g


def _sigmoid(x):
    return 1.0 / (1.0 + jnp.exp(-x))


def _silu(x):
    return x * _sigmoid(x)


def _softplus(x):
    return jnp.maximum(x, 0.0) + jnp.log1p(jnp.exp(-jnp.abs(x)))


def _pad_rows(a, rows):
    if a.shape[0] == rows:
        return a
    return jnp.concatenate([a, jnp.zeros((rows - a.shape[0],) + a.shape[1:], a.dtype)], axis=0)


def _causal_conv(ubuf, cw_ref, cb_ref, tc):
    acc = cb_ref[...] + cw_ref[CONV_K - 1:CONV_K, :] * ubuf[CONV_BASE:CONV_BASE + tc, :]
    for k in range(CONV_K - 1):
        off = CONV_BASE - (CONV_K - 1) + k
        acc = acc + cw_ref[k:k + 1, :] * ubuf[off:off + tc, :]
    return acc


def _conv_roll_history(ubuf, convn_ref, tc):
    tail = ubuf[CONV_BASE + tc - (CONV_K - 1):CONV_BASE + tc, :]
    ubuf[CONV_BASE - (CONV_K - 1):CONV_BASE, :] = tail
    convn_ref[0] = tail


def _cum_matrices(lq, lk):
    t_idx = lax.broadcasted_iota(jnp.int32, (lq, lk), 0)
    s_idx = lax.broadcasted_iota(jnp.int32, (lq, lk), 1)
    causal = s_idx <= t_idx
    r_idx = lax.broadcasted_iota(jnp.int32, (lk, lk), 0)
    c_idx = lax.broadcasted_iota(jnp.int32, (lk, lk), 1)
    upper = (r_idx <= c_idx).astype(F32)
    return causal, causal.astype(F32), upper


def _lru_kernel(*refs, tc, has_init):
    it = iter(refs)
    x_ref = next(it)
    if has_init:
        conv0_ref, h0_ref = next(it), next(it)
    (gpre_ref, gpost_ref, win_ref, cw_ref, cb_ref, wax_ref, ba_ref, bx_ref, lam_ref,
     wout_ref) = (next(it) for _ in range(10))
    y_ref, convn_ref, hn_ref = next(it), next(it), next(it)
    ubuf, a_s, b_s = next(it), next(it), next(it)
    rdim = a_s.shape[1]
    nblk, bs, _ = wax_ref.shape

    @pl.when(pl.program_id(1) == 0)
    def _():
        if has_init:
            ubuf[CONV_BASE - (CONV_K - 1):CONV_BASE, :] = conv0_ref[0]
            hn_ref[0] = h0_ref[0]
        else:
            ubuf[CONV_BASE - (CONV_K - 1):CONV_BASE, :] = jnp.zeros((CONV_K - 1, rdim), F32)
            hn_ref[0] = jnp.zeros((1, rdim), F32)

    x = x_ref[0]
    xn = _rmsnorm(x, gpre_ref[...])
    ug = _mm(xn, win_ref[...])
    ubuf[CONV_BASE:CONV_BASE + tc, :] = ug[:, :rdim]
    gate = ug[:, rdim:]
    uc = _causal_conv(ubuf, cw_ref, cb_ref, tc)
    _conv_roll_history(ubuf, convn_ref, tc)

    sp = _softplus(-lam_ref[...])
    for n in range(nblk):
        sl = slice(n * bs, (n + 1) * bs)
        ucn = uc[:, sl]
        res = _mm(ucn, wax_ref[n])
        r = _sigmoid(res[:, :bs] + ba_ref[:, sl])
        i = _sigmoid(res[:, bs:] + bx_ref[:, sl])
        log_a = (-LRU_C) * r * sp[:, sl]
        a = jnp.exp(log_a)
        a_s[:, sl] = a
        b_s[:, sl] = jnp.sqrt(1.0 - a * a) * (i * ucn)

    row = lax.broadcasted_iota(jnp.int32, (SUBLANES, rdim), 0)

    def group(gi, carry):
        r0 = pl.multiple_of(gi * SUBLANES, SUBLANES)
        a8 = a_s[pl.ds(r0, SUBLANES), :]
        b8 = b_s[pl.ds(r0, SUBLANES), :]
        for d in (1, 2, 4):
            keep = row >= d
            b_sh = jnp.where(keep, pltpu.roll(b8, d, 0), 0.0)
            a_sh = jnp.where(keep, pltpu.roll(a8, d, 0), 1.0)
            b8 = b8 + a8 * b_sh
            a8 = a8 * a_sh
        h8 = b8 + a8 * carry
        b_s[pl.ds(r0, SUBLANES), :] = h8
        return h8[SUBLANES - 1:SUBLANES, :]

    h_last = lax.fori_loop(0, tc // SUBLANES, group, hn_ref[0])
    hn_ref[0] = h_last

    y = _mm(b_s[...] * _silu(gate), wout_ref[...])
    y_ref[0] = x + _rmsnorm(y, gpost_ref[...])


def _vmem_spec():
    return pl.BlockSpec(memory_space=pltpu.VMEM)


def _lru_layer(x, state, p, tc):
    bsz, tlen, dm = x.shape
    rdim = p["w_out"].shape[0]
    has_init = state is not None
    grid = (bsz, tlen // tc)
    in_specs = [pl.BlockSpec((1, tc, dm), lambda b, t: (b, t, 0))]
    args = [x]
    if has_init:
        conv0, h0 = state
        in_specs += [pl.BlockSpec((1, CONV_K - 1, rdim), lambda b, t: (b, 0, 0)),
                     pl.BlockSpec((1, 1, rdim), lambda b, t: (b, 0, 0))]
        args += [conv0, h0.reshape(bsz, 1, rdim)]
    weights = [p["g_pre"], p["g_post"], p["w_in"], p["conv_w"], p["conv_b"], p["w_ax"],
               p["b_a"], p["b_x"], p["lam"], p["w_out"]]
    in_specs += [_vmem_spec() for _ in weights]
    args += weights
    out_shape = (jax.ShapeDtypeStruct((bsz, tlen, dm), F32),
                 jax.ShapeDtypeStruct((bsz, CONV_K - 1, rdim), F32),
                 jax.ShapeDtypeStruct((bsz, 1, rdim), F32))
    out_specs = (pl.BlockSpec((1, tc, dm), lambda b, t: (b, t, 0)),
                 pl.BlockSpec((1, CONV_K - 1, rdim), lambda b, t: (b, 0, 0)),
                 pl.BlockSpec((1, 1, rdim), lambda b, t: (b, 0, 0)))
    scratch = [pltpu.VMEM((CONV_BASE + tc, rdim), F32),
               pltpu.VMEM((tc, rdim), F32),
               pltpu.VMEM((tc, rdim), F32)]
    y, convn, hn = pl.pallas_call(
        functools.partial(_lru_kernel, tc=tc, has_init=has_init),
        grid=grid, in_specs=in_specs, out_specs=out_specs, out_shape=out_shape,
        scratch_shapes=scratch, name="lru_layer",
        compiler_params=pltpu.CompilerParams(
            dimension_semantics=("arbitrary", "arbitrary"),
            vmem_limit_bytes=VMEM_LIMIT_BYTES),
    )(*args)
    return y, (convn, hn.reshape(bsz, rdim))


def _mlstm_kernel(*refs, tc, has_init, heads):
    it = iter(refs)
    x_ref = next(it)
    if has_init:
        conv0_ref, c0_ref, n0_ref, m0_ref = (next(it) for _ in range(4))
    (gpre_ref, gpost_ref, win_ref, cw_ref, cb_ref, wqk_ref, wv_ref, wg_ref, bg_ref,
     lnw_ref, skip_ref, wout_ref) = (next(it) for _ in range(12))
    y_ref, convn_ref, cn_ref, nn_ref, mn_ref = (next(it) for _ in range(5))
    ubuf, q_s, k_s, v_s, h_s = (next(it) for _ in range(5))
    rdim = q_s.shape[1]
    dh = rdim // heads
    nblk, bs, _ = wv_ref.shape
    lk = max(tc, LANES)

    @pl.when(pl.program_id(1) == 0)
    def _():
        if has_init:
            ubuf[CONV_BASE - (CONV_K - 1):CONV_BASE, :] = conv0_ref[0]
            cn_ref[...] = c0_ref[...]
            nn_ref[...] = n0_ref[...]
            mn_ref[...] = m0_ref[...]
        else:
            ubuf[CONV_BASE - (CONV_K - 1):CONV_BASE, :] = jnp.zeros((CONV_K - 1, rdim), F32)
            cn_ref[...] = jnp.zeros(cn_ref.shape, F32)
            nn_ref[...] = jnp.zeros(nn_ref.shape, F32)
            mn_ref[...] = jnp.zeros(mn_ref.shape, F32)

    x = x_ref[0]
    xn = _rmsnorm(x, gpre_ref[...])
    uz = _mm(xn, win_ref[...])
    u = uz[:, :rdim]
    z = uz[:, rdim:]
    ubuf[CONV_BASE:CONV_BASE + tc, :] = u
    uc = _silu(_causal_conv(ubuf, cw_ref, cb_ref, tc))
    _conv_roll_history(ubuf, convn_ref, tc)

    for n in range(nblk):
        sl = slice(n * bs, (n + 1) * bs)
        res = _mm(uc[:, sl], wqk_ref[n])
        q_s[:, sl] = res[:, :bs]
        k_s[:, sl] = res[:, bs:]
        v_s[:, sl] = _mm(u[:, sl], wv_ref[n])

    gates = (_mm(q_s[...], wg_ref[0:rdim, :]) + _mm(k_s[...], wg_ref[rdim:2 * rdim, :]) +
             _mm(v_s[...], wg_ref[2 * rdim:3 * rdim, :]) + bg_ref[...])
    logf = jnp.minimum(gates, 0.0) - jnp.log1p(jnp.exp(-jnp.abs(gates)))
    causal, tri, upper = _cum_matrices(tc, lk)
    logf_p = _pad_rows(logf, lk)
    bcol = _mm_f32(tri, logf_p)
    brow = _mm_f32(_transpose_f32(logf_p), upper)
    grow = _transpose_f32(_pad_rows(gates, lk))
    lane = lax.broadcasted_iota(jnp.int32, (1, LANES), 1)
    m_row = mn_ref[0]
    m_row_new = m_row

    for h in range(heads):
        hs = slice(h * dh, (h + 1) * dh)
        fh = heads + h
        bc = bcol[:, fh:fh + 1]
        br = brow[fh:fh + 1, :]
        ir = grow[h:h + 1, :]
        ic = gates[:, h:h + 1]
        m_prev = m_row[:, h:h + 1]
        qh = q_s[:, hs]
        kh = k_s[:, hs]
        vh = v_s[:, hs]
        kh_p = _pad_rows(kh, lk)
        vh_p = _pad_rows(vh, lk)
        log_d = jnp.where(causal, bc - br + ir, NEG_INF)
        log_inter = bc + m_prev
        mt = jnp.maximum(log_inter, jnp.max(log_d, axis=-1, keepdims=True))
        s_mat = _mm_nt(qh, kh_p) * jnp.exp(log_d - mt)
        w_int = jnp.exp(log_inter - mt)
        c_h = cn_ref[0, h]
        n_h = nn_ref[0, h:h + 1, :]
        num = _mm(s_mat, vh_p) + w_int * _mm(qh, c_h)
        den = (jnp.sum(s_mat, axis=-1, keepdims=True) +
               w_int * jnp.sum(qh * n_h, axis=-1, keepdims=True))
        hc = num / jnp.maximum(jnp.abs(den), jnp.exp(-mt))
        mu = jnp.mean(hc, axis=-1, keepdims=True)
        cen = hc - mu
        var = jnp.mean(cen * cen, axis=-1, keepdims=True)
        h_s[:, hs] = cen * lax.rsqrt(var + EPS)
        m_new = mt[tc - 1:tc, :]
        b_last = bc[tc - 1:tc, :]
        w_state = jnp.exp(b_last - bc + ic - m_new)
        decay = jnp.exp(b_last + m_prev - m_new)
        kw = kh * w_state
        cn_ref[0, h] = decay * c_h + _mm_tn(_pad_rows(kw, lk), vh_p)
        nn_ref[0, h:h + 1, :] = decay * n_h + jnp.sum(kw, axis=0, keepdims=True)
        m_row_new = jnp.where(lane == h, m_new, m_row_new)

    mn_ref[0] = m_row_new
    hn = h_s[...] * lnw_ref[...]
    y = _mm((hn + skip_ref[...] * uc) * _silu(z), wout_ref[...])
    y_ref[0] = x + _rmsnorm(y, gpost_ref[...])


def _mlstm_layer(x, state, p, tc):
    bsz, tlen, dm = x.shape
    rdim = p["w_out"].shape[0]
    heads = p["heads"]
    dh = rdim // heads
    has_init = state is not None
    grid = (bsz, tlen // tc)
    in_specs = [pl.BlockSpec((1, tc, dm), lambda b, t: (b, t, 0))]
    args = [x]
    state_specs = [pl.BlockSpec((1, CONV_K - 1, rdim), lambda b, t: (b, 0, 0)),
                   pl.BlockSpec((1, heads, dh, dh), lambda b, t: (b, 0, 0, 0)),
                   pl.BlockSpec((1, heads, dh), lambda b, t: (b, 0, 0)),
                   pl.BlockSpec((1, 1, LANES), lambda b, t: (b, 0, 0))]
    if has_init:
        conv0, c0, n0, m0 = state
        m0p = jnp.pad(m0, ((0, 0), (0, LANES - heads))).reshape(bsz, 1, LANES)
        in_specs += state_specs
        args += [conv0, c0, n0, m0p]
    weights = [p["g_pre"], p["g_post"], p["w_in"], p["conv_w"], p["conv_b"], p["w_qk"],
               p["w_v"], p["w_g"], p["b_g"], p["ln_w"], p["skip"], p["w_out"]]
    in_specs += [_vmem_spec() for _ in weights]
    args += weights
    out_shape = (jax.ShapeDtypeStruct((bsz, tlen, dm), F32),
                 jax.ShapeDtypeStruct((bsz, CONV_K - 1, rdim), F32),
                 jax.ShapeDtypeStruct((bsz, heads, dh, dh), F32),
                 jax.ShapeDtypeStruct((bsz, heads, dh), F32),
                 jax.ShapeDtypeStruct((bsz, 1, LANES), F32))
    out_specs = tuple([pl.BlockSpec((1, tc, dm), lambda b, t: (b, t, 0))] + state_specs)
    scratch = [pltpu.VMEM((CONV_BASE + tc, rdim), F32)] + [pltpu.VMEM((tc, rdim), F32)] * 4
    y, convn, cn, nn, mn = pl.pallas_call(
        functools.partial(_mlstm_kernel, tc=tc, has_init=has_init, heads=heads),
        grid=grid, in_specs=in_specs, out_specs=out_specs, out_shape=out_shape,
        scratch_shapes=scratch, name="mlstm_layer",
        compiler_params=pltpu.CompilerParams(
            dimension_semantics=("arbitrary", "arbitrary"),
            vmem_limit_bytes=VMEM_LIMIT_BYTES),
    )(*args)
    return y, (convn, cn, nn, mn[:, 0, :heads])


def _ssd_kernel(*refs, tc, has_init, groups, dstate, headdim):
    it = iter(refs)
    x_ref = next(it)
    if has_init:
        conv0_ref, s0_ref = next(it), next(it)
    (gpre_ref, gpost_ref, wzx_ref, wdt_ref, cw_ref, cb_ref, dtb_ref, alog_ref, dskip_ref,
     normw_ref, wout_ref) = (next(it) for _ in range(11))
    y_ref, convn_ref, sn_ref = next(it), next(it), next(it)
    ubuf, xbc_s, y_s = next(it), next(it), next(it)
    inner = y_s.shape[1]
    nheads = inner // headdim
    hpg = nheads // groups
    gw = inner // groups
    per_vreg = LANES // headdim
    lk = max(tc, LANES)
    cdim = xbc_s.shape[1]

    @pl.when(pl.program_id(1) == 0)
    def _():
        if has_init:
            ubuf[CONV_BASE - (CONV_K - 1):CONV_BASE, :] = conv0_ref[0]
            sn_ref[...] = s0_ref[...]
        else:
            ubuf[CONV_BASE - (CONV_K - 1):CONV_BASE, :] = jnp.zeros((CONV_K - 1, cdim), F32)
            sn_ref[...] = jnp.zeros(sn_ref.shape, F32)

    x = x_ref[0]
    xn = _rmsnorm(x, gpre_ref[...])
    proj = _mm(xn, wzx_ref[...])
    z = proj[:, :inner]
    ubuf[CONV_BASE:CONV_BASE + tc, :] = proj[:, inner:]
    xbc_s[...] = _silu(_causal_conv(ubuf, cw_ref, cb_ref, tc))
    _conv_roll_history(ubuf, convn_ref, tc)

    dt = _softplus(_mm(xn, wdt_ref[...]) + dtb_ref[...])
    a_neg = -jnp.exp(alog_ref[...])
    dta = dt * a_neg
    causal, tri, upper = _cum_matrices(tc, lk)
    dta_p = _pad_rows(dta, lk)
    cs = _mm_f32(tri, dta_p)
    cs_row = _mm_f32(_transpose_f32(dta_p), upper)
    ecs = jnp.exp(cs)
    cs_last = cs[tc - 1:tc, :]
    ecs_last = jnp.exp(cs_last)
    wcol = jnp.exp(cs_last - cs) * dt
    lane = lax.broadcasted_iota(jnp.int32, (1, LANES), 1)

    def spread(col, h0):
        out = col[:, h0:h0 + 1]
        for j in range(1, per_vreg):
            out = jnp.where(lane >= j * headdim, col[:, h0 + j:h0 + j + 1], out)
        return out

    for g in range(groups):
        b_g = xbc_s[:, inner + g * dstate:inner + (g + 1) * dstate]
        c_g = xbc_s[:, inner + (groups + g) * dstate:inner + (groups + g + 1) * dstate]
        b_gp = _pad_rows(b_g, lk)
        cb = _mm_nt(c_g, b_gp)
        s_g = sn_ref[0, g * hpg:(g + 1) * hpg].reshape(hpg * headdim, dstate)
        c_state = _mm_nt(c_g, s_g)
        for pr in range(hpg // per_vreg):
            h0 = g * hpg + pr * per_vreg
            ls = slice(g * gw + pr * LANES, g * gw + (pr + 1) * LANES)
            xs_t = xbc_s[:, ls]
            xdt_p = _pad_rows(xs_t * spread(dt, h0), lk)
            y_t = spread(ecs, h0) * c_state[:, pr * LANES:(pr + 1) * LANES]
            for j in range(per_vreg):
                hh = h0 + j
                dec = jnp.exp(jnp.where(causal, cs[:, hh:hh + 1] - cs_row[hh:hh + 1, :], NEG_INF))
                in_head = (lane >= j * headdim) & (lane < (j + 1) * headdim)
                y_t = y_t + _mm(cb * dec, jnp.where(in_head, xdt_p, 0.0))
            y_s[:, ls] = y_t
            upd = _mm_tn(_pad_rows(xs_t * spread(wcol, h0), lk), b_gp)
            for j in range(per_vreg):
                hh = h0 + j
                sn_ref[0, hh] = (ecs_last[:, hh:hh + 1] * sn_ref[0, hh] +
                                 upd[j * headdim:(j + 1) * headdim, :])

    y = (y_s[...] + dskip_ref[...] * xbc_s[:, :inner]) * _silu(z)
    for g in range(groups):
        gs = slice(g * gw, (g + 1) * gw)
        yg = y[:, gs]
        y_s[:, gs] = yg * lax.rsqrt(jnp.mean(yg * yg, axis=-1, keepdims=True) + EPS)
    out = _mm(y_s[...] * normw_ref[...], wout_ref[...])
    y_ref[0] = x + _rmsnorm(out, gpost_ref[...])


def _ssd_layer(x, state, p, tc):
    bsz, tlen, dm = x.shape
    inner = p["w_out"].shape[0]
    groups, dstate, headdim = p["groups"], p["dstate"], p["headdim"]
    nheads = inner // headdim
    cdim = inner + 2 * groups * dstate
    has_init = state is not None
    grid = (bsz, tlen // tc)
    in_specs = [pl.BlockSpec((1, tc, dm), lambda b, t: (b, t, 0))]
    args = [x]
    state_specs = [pl.BlockSpec((1, CONV_K - 1, cdim), lambda b, t: (b, 0, 0)),
                   pl.BlockSpec((1, nheads, headdim, dstate), lambda b, t: (b, 0, 0, 0))]
    if has_init:
        in_specs += state_specs
        args += list(state)
    weights = [p["g_pre"], p["g_post"], p["w_zx"], p["w_dt"], p["conv_w"], p["conv_b"],
               p["dt_bias"], p["a_log"], p["d_skip"], p["norm_w"], p["w_out"]]
    in_specs += [_vmem_spec() for _ in weights]
    args += weights
    out_shape = (jax.ShapeDtypeStruct((bsz, tlen, dm), F32),
                 jax.ShapeDtypeStruct((bsz, CONV_K - 1, cdim), F32),
                 jax.ShapeDtypeStruct((bsz, nheads, headdim, dstate), F32))
    out_specs = tuple([pl.BlockSpec((1, tc, dm), lambda b, t: (b, t, 0))] + state_specs)
    scratch = [pltpu.VMEM((CONV_BASE + tc, cdim), F32),
               pltpu.VMEM((tc, cdim), F32),
               pltpu.VMEM((tc, inner), F32)]
    y, convn, sn = pl.pallas_call(
        functools.partial(_ssd_kernel, tc=tc, has_init=has_init, groups=groups,
                          dstate=dstate, headdim=headdim),
        grid=grid, in_specs=in_specs, out_specs=out_specs, out_shape=out_shape,
        scratch_shapes=scratch, name="ssd_layer",
        compiler_params=pltpu.CompilerParams(
            dimension_semantics=("arbitrary", "arbitrary"),
            vmem_limit_bytes=VMEM_LIMIT_BYTES),
    )(*args)
    return y, (convn, sn)


def _row(v, width=None):
    v = v.astype(F32).reshape(1, -1)
    if width is not None and v.shape[1] < width:
        v = jnp.pad(v, ((0, 0), (0, width - v.shape[1])))
    return v


def _expand_block_diag(w, bs_out):
    nb, s, _ = w.shape
    per = bs_out // s
    w4 = w.reshape(nb // per, per, s, s)
    eye = jnp.eye(per, dtype=w.dtype)
    return jnp.einsum("nbij,bc->nbicj", w4, eye).reshape(nb // per, bs_out, bs_out)


def _lru_params(j, g_pre, g_post, w_in, conv_w, conv_b, w_a, b_a, w_x, b_x, lam, w_out):
    return dict(g_pre=_row(g_pre), g_post=_row(g_post), w_in=w_in[j].astype(BF16),
                conv_w=conv_w[j].astype(F32), conv_b=_row(conv_b[j]),
                w_ax=jnp.concatenate([w_a[j], w_x[j]], axis=-1).astype(BF16),
                b_a=_row(b_a[j]), b_x=_row(b_x[j]), lam=_row(lam[j]),
                w_out=w_out[j].astype(BF16))


def _mlstm_params(j, g_pre, g_post, w_in, conv_w, conv_b, w_q, w_k, w_v, w_ig, b_ig, w_fg,
                  b_fg, ln_w, skip, w_out):
    heads = w_ig.shape[-1]
    rdim = w_out.shape[1]
    dh = rdim // heads
    wq = _expand_block_diag(w_q[j], LANES)
    wk = _expand_block_diag(w_k[j], LANES) * (dh ** -0.5)
    wv = _expand_block_diag(w_v[j], LANES)
    w_g = jnp.concatenate([w_ig[j], w_fg[j]], axis=-1)
    w_g = jnp.pad(w_g, ((0, 0), (0, LANES - 2 * heads)))
    b_g = _row(jnp.concatenate([b_ig[j], b_fg[j]]), LANES)
    return dict(g_pre=_row(g_pre), g_post=_row(g_post), w_in=w_in[j].astype(BF16),
                conv_w=conv_w[j].astype(F32), conv_b=_row(conv_b[j]),
                w_qk=jnp.concatenate([wq, wk], axis=-1).astype(BF16), w_v=wv.astype(BF16),
                w_g=w_g.astype(BF16), b_g=b_g, ln_w=_row(ln_w[j]), skip=_row(skip[j]),
                w_out=w_out[j].astype(BF16), heads=heads)


def _ssd_params(j, g_pre, g_post, w_in, conv_w, conv_b, dt_bias, a_log, d_skip, norm_w, w_out,
                groups, dstate):
    inner = w_out.shape[1]
    nheads = dt_bias.shape[-1]
    headdim = inner // nheads
    cdim = conv_w.shape[-1]
    w = w_in[j]
    w_dt = jnp.pad(w[:, inner + cdim:], ((0, 0), (0, LANES - nheads)))
    return dict(g_pre=_row(g_pre), g_post=_row(g_post), w_zx=w[:, :inner + cdim].astype(BF16),
                w_dt=w_dt.astype(BF16), conv_w=conv_w[j].astype(F32), conv_b=_row(conv_b[j]),
                dt_bias=_row(dt_bias[j], LANES), a_log=_row(a_log[j], LANES),
                d_skip=_row(jnp.repeat(d_skip[j], headdim)), norm_w=_row(norm_w[j]),
                w_out=w_out[j].astype(BF16), groups=groups, dstate=dstate, headdim=headdim)


SSD_GROUPS = 4
SSD_DSTATE = 128
PROMPT_CHUNK = 256


def kernel(x_prompt, x_sample, state_l0_conv, state_l0_h, state_l1_conv, state_l1_C, state_l1_n, state_l1_m, state_l2_conv, state_l2_ssm, state_l3_conv, state_l3_h, norm_pre, norm_post, lru_w_in, lru_conv_w, lru_conv_b, lru_w_a, lru_b_a, lru_w_x, lru_b_x, lru_lambda, lru_w_out, mlstm_w_in, mlstm_conv_w, mlstm_conv_b, mlstm_w_q, mlstm_w_k, mlstm_w_v, mlstm_w_ig, mlstm_b_ig, mlstm_w_fg, mlstm_b_fg, mlstm_ln_w, mlstm_skip, mlstm_w_out, ssd_w_in, ssd_conv_w, ssd_conv_b, ssd_dt_bias, ssd_a_log, ssd_d_skip, ssd_norm_w, ssd_w_out):
    lru_w = (lru_w_in, lru_conv_w, lru_conv_b, lru_w_a, lru_b_a, lru_w_x, lru_b_x, lru_lambda,
             lru_w_out)
    p0 = _lru_params(0, norm_pre[0], norm_post[0], *lru_w)
    p1 = _mlstm_params(0, norm_pre[1], norm_post[1], mlstm_w_in, mlstm_conv_w, mlstm_conv_b,
                       mlstm_w_q, mlstm_w_k, mlstm_w_v, mlstm_w_ig, mlstm_b_ig, mlstm_w_fg,
                       mlstm_b_fg, mlstm_ln_w, mlstm_skip, mlstm_w_out)
    p2 = _ssd_params(0, norm_pre[2], norm_post[2], ssd_w_in, ssd_conv_w, ssd_conv_b, ssd_dt_bias,
                     ssd_a_log, ssd_d_skip, ssd_norm_w, ssd_w_out, SSD_GROUPS, SSD_DSTATE)
    p3 = _lru_params(1, norm_pre[3], norm_post[3], *lru_w)

    def trunk(x, states, tc):
        st = states if states is not None else [None] * 4
        x, s0 = _lru_layer(x, st[0], p0, tc)
        x, s1 = _mlstm_layer(x, st[1], p1, tc)
        x, s2 = _ssd_layer(x, st[2], p2, tc)
        x, s3 = _lru_layer(x, st[3], p3, tc)
        return x, (*s0, *s1, *s2, *s3)

    sample_states = [(state_l0_conv, state_l0_h),
                     (state_l1_conv, state_l1_C, state_l1_n, state_l1_m),
                     (state_l2_conv, state_l2_ssm),
                     (state_l3_conv, state_l3_h)]
    y_prompt, p_states = trunk(x_prompt, None, min(PROMPT_CHUNK, x_prompt.shape[1]))
    y_sample, s_states = trunk(x_sample, sample_states, x_sample.shape[1])
    return (y_prompt, y_sample, *p_states, *s_states)
```

```python
import functools

import jax
import jax.numpy as jnp
from jax import lax
from jax.experimental import pallas as pl
from jax.experimental.pallas import tpu as pltpu

F32 = jnp.float32
BF16 = jnp.bfloat16
EPS = 1e-6
CONV_K = 4
LRU_C = 8.0
LANES = 128
SUBLANES = 8
CONV_BASE = (CONV_K - 1) * SUBLANES
VMEM_LIMIT_BYTES = 58 * 1024 * 1024
NEG_INF = float("-inf")
FAR_FUTURE = 1 << 30
TINY = 1e-30
GATE_PIECES = 4
LOG2E = 1.4426950408889634


def _mm(a, b):
    return jnp.dot(a.astype(BF16), b.astype(BF16), preferred_element_type=F32)


def _mmw(a, w_packed):
    return jnp.dot(a.astype(BF16), pltpu.bitcast(w_packed, BF16), preferred_element_type=F32)


def _mm_nt(a, b):
    return lax.dot_general(a.astype(BF16), b.astype(BF16), (((1,), (1,)), ((), ())),
                           preferred_element_type=F32)


def _mm_tn(a, b):
    return lax.dot_general(a.astype(BF16), b.astype(BF16), (((0,), (0,)), ((), ())),
                           preferred_element_type=F32)


def _mm_f32(a, b):
    return jnp.dot(a, b, precision=lax.Precision.HIGHEST, preferred_element_type=F32)


def _transpose_f32(a):
    n = a.shape[1]
    eye = (lax.broadcasted_iota(jnp.int32, (n, n), 0) ==
           lax.broadcasted_iota(jnp.int32, (n, n), 1)).astype(F32)
    return lax.dot_general(eye, a, (((1,), (1,)), ((), ())),
                           precision=lax.Precision.HIGHEST, preferred_element_type=F32)


def _rmsnorm(x, g):
    return x * lax.rsqrt(jnp.mean(x * x, axis=-1, keepdims=True) + EPS) * g


def _sigmoid(x):
    return jax.nn.sigmoid(x)


def _silu(x):
    return x * _sigmoid(x)


def _softplus(x):
    return jnp.maximum(x, 0.0) + jnp.log1p(jnp.exp(-jnp.abs(x)))


def _pad_rows(a, rows):
    if a.shape[0] == rows:
        return a
    return jnp.concatenate([a, jnp.zeros((rows - a.shape[0],) + a.shape[1:], a.dtype)], axis=0)


def _row_of_time(t, tc):
    seg = tc // SUBLANES
    return (t % seg) * SUBLANES + t // seg


def _causal_conv(ubuf, hist_ref, cw_ref, cb_ref, tc):
    seg = tc // SUBLANES
    width = ubuf.shape[1]
    sub = lax.broadcasted_iota(jnp.int32, (SUBLANES, width), 0)
    for k in range(1, CONV_K):
        shift = -(-k // seg)
        src = CONV_BASE + (shift * seg - k) * SUBLANES
        blk = pltpu.roll(ubuf[src:src + SUBLANES, :], shift, 0)
        for s in range(shift):
            h = (CONV_K - 1) + s * seg - k
            blk = jnp.where(sub == s, hist_ref[0, h:h + 1, :], blk)
        ubuf[CONV_BASE - k * SUBLANES:CONV_BASE - (k - 1) * SUBLANES, :] = blk
    acc = cb_ref[...] + cw_ref[CONV_K - 1:CONV_K, :] * ubuf[CONV_BASE:CONV_BASE + tc, :]
    for k in range(1, CONV_K):
        off = CONV_BASE - k * SUBLANES
        acc = acc + cw_ref[CONV_K - 1 - k:CONV_K - k, :] * ubuf[off:off + tc, :]
    return acc


def _conv_save_history(ubuf, hist_ref, tc):
    for i in range(CONV_K - 1):
        r = CONV_BASE + _row_of_time(tc - (CONV_K - 1) + i, tc)
        hist_ref[0, i:i + 1, :] = ubuf[r:r + 1, :]


def _time_index(shape, dim, tc):
    r = lax.broadcasted_iota(jnp.int32, shape, dim)
    t = (r & (SUBLANES - 1)) * (tc // SUBLANES) + (r >> 3)
    if shape[dim] > tc:
        t = jnp.where(r < tc, t, FAR_FUTURE)
    return t


def _cum_matrices(lq, lk):
    causal = _time_index((lq, lk), 1, lq) <= _time_index((lq, lk), 0, lq)
    upper = (_time_index((lk, lk), 0, lq) <= _time_index((lk, lk), 1, lq)).astype(F32)
    return causal, causal.astype(F32), upper


def _lru_kernel(*refs, tc, has_init):
    it = iter(refs)
    x_ref = next(it)
    if has_init:
        conv0_ref, h0_ref = next(it), next(it)
    (gpre_ref, gpost_ref, win_ref, cw_ref, cb_ref, wax_ref, ba_ref, bx_ref, lam_ref,
     wout_ref) = (next(it) for _ in range(10))
    y_ref, convn_ref, hn_ref = next(it), next(it), next(it)
    ubuf, a_s, b_s, g_s = next(it), next(it), next(it), next(it)
    rdim = a_s.shape[1]
    nblk = wax_ref.shape[0]
    bs = wax_ref.shape[2] // 2

    @pl.when(pl.program_id(1) == 0)
    def _():
        if has_init:
            convn_ref[...] = conv0_ref[...]
            hn_ref[0] = h0_ref[0]
        else:
            convn_ref[...] = jnp.zeros(convn_ref.shape, F32)
            hn_ref[0] = jnp.zeros((1, rdim), F32)

    x = x_ref[0]
    xn = _rmsnorm(x, gpre_ref[...]).astype(BF16)
    ubuf[CONV_BASE:CONV_BASE + tc, :] = _mmw(xn, win_ref[:, :rdim])
    uc = _causal_conv(ubuf, convn_ref, cw_ref, cb_ref, tc)
    _conv_save_history(ubuf, convn_ref, tc)

    rate = (-LRU_C * LOG2E) * _softplus(-lam_ref[...])
    gate_cols = rdim // GATE_PIECES
    blocks_per_piece = nblk // GATE_PIECES
    for n in range(nblk):
        sl = slice(n * bs, (n + 1) * bs)
        ucn = uc[:, sl]
        res = _mmw(ucn, wax_ref[n])
        r = _sigmoid(res[:, :bs] + ba_ref[:, sl])
        i = _sigmoid(res[:, bs:] + bx_ref[:, sl])
        a = jnp.exp2(r * rate[:, sl])
        gain2 = 1.0 - a * a
        a_s[:, sl] = a
        b_s[:, sl] = (gain2 * lax.rsqrt(jnp.maximum(gain2, TINY))) * (i * ucn)
        if n % blocks_per_piece == blocks_per_piece - 1:
            gs = slice((n // blocks_per_piece) * gate_cols, (n // blocks_per_piece + 1) * gate_cols)
            g_s[:, gs] = _silu(_mmw(xn, win_ref[:, rdim + gs.start:rdim + gs.stop]))

    carry_in, h_last = _segment_scan(a_s, b_s, hn_ref[0], tc)
    hn_ref[0] = h_last
    seg = tc // SUBLANES
    h = (b_s[...].reshape(seg, SUBLANES, rdim) +
         a_s[...].reshape(seg, SUBLANES, rdim) * carry_in[None]).reshape(tc, rdim)
    y = _mmw(h * g_s[...], wout_ref[...])
    y_ref[0] = x + _rmsnorm(y, gpost_ref[...])


def _segment_scan(a_s, b_s, h_prev, tc):
    seg = tc // SUBLANES
    rdim = a_s.shape[1]
    half = rdim // 2
    ends = []
    for c0 in (0, half):
        def step(j, carry, c0=c0):
            h, prod = carry
            r0 = pl.multiple_of(j * SUBLANES, SUBLANES)
            a8 = a_s[pl.ds(r0, SUBLANES), c0:c0 + half]
            h = a8 * h + b_s[pl.ds(r0, SUBLANES), c0:c0 + half]
            prod = a8 * prod
            b_s[pl.ds(r0, SUBLANES), c0:c0 + half] = h
            a_s[pl.ds(r0, SUBLANES), c0:c0 + half] = prod
            return h, prod
        ends.append(lax.fori_loop(0, seg, step, (jnp.zeros((SUBLANES, half), F32),
                                                 jnp.ones((SUBLANES, half), F32))))
    h_end = jnp.concatenate([ends[0][0], ends[1][0]], axis=1)
    p_end = jnp.concatenate([ends[0][1], ends[1][1]], axis=1)
    sub = lax.broadcasted_iota(jnp.int32, (SUBLANES, rdim), 0)
    carry_in = jnp.broadcast_to(h_prev, (SUBLANES, rdim))
    for s in range(1, SUBLANES):
        carry_in = jnp.where(sub == s, pltpu.roll(h_end + p_end * carry_in, 1, 0), carry_in)
    h_last = (h_end + p_end * carry_in)[SUBLANES - 1:SUBLANES, :]
    return carry_in, h_last


def _vmem_spec():
    return pl.BlockSpec(memory_space=pltpu.VMEM)


def _lru_layer(x, state, p, tc):
    bsz, tlen, dm = x.shape
    rdim = p["rdim"]
    has_init = state is not None
    grid = (bsz, tlen // tc)
    in_specs = [pl.BlockSpec((1, tc, dm), lambda b, t: (b, t, 0))]
    args = [x]
    if has_init:
        conv0, h0 = state
        in_specs += [pl.BlockSpec((1, CONV_K - 1, rdim), lambda b, t: (b, 0, 0)),
                     pl.BlockSpec((1, 1, rdim), lambda b, t: (b, 0, 0))]
        args += [conv0, h0.reshape(bsz, 1, rdim)]
    weights = [p["g_pre"], p["g_post"], p["w_in"], p["conv_w"], p["conv_b"], p["w_ax"],
               p["b_a"], p["b_x"], p["lam"], p["w_out"]]
    in_specs += [_vmem_spec() for _ in weights]
    args += weights
    out_shape = (jax.ShapeDtypeStruct((bsz, tlen, dm), F32),
                 jax.ShapeDtypeStruct((bsz, CONV_K - 1, rdim), F32),
                 jax.ShapeDtypeStruct((bsz, 1, rdim), F32))
    out_specs = (pl.BlockSpec((1, tc, dm), lambda b, t: (b, t, 0)),
                 pl.BlockSpec((1, CONV_K - 1, rdim), lambda b, t: (b, 0, 0)),
                 pl.BlockSpec((1, 1, rdim), lambda b, t: (b, 0, 0)))
    scratch = [pltpu.VMEM((CONV_BASE + tc, rdim), F32)] + [pltpu.VMEM((tc, rdim), F32)] * 3
    y, convn, hn = pl.pallas_call(
        functools.partial(_lru_kernel, tc=tc, has_init=has_init),
        grid=grid, in_specs=in_specs, out_specs=out_specs, out_shape=out_shape,
        scratch_shapes=scratch, name="lru_layer",
        compiler_params=pltpu.CompilerParams(
            dimension_semantics=("arbitrary", "arbitrary"),
            vmem_limit_bytes=VMEM_LIMIT_BYTES),
    )(*args)
    return y, (convn, hn.reshape(bsz, rdim))


def _mlstm_kernel(*refs, tc, has_init, heads):
    it = iter(refs)
    x_ref = next(it)
    if has_init:
        conv0_ref, c0_ref, n0_ref, m0_ref = (next(it) for _ in range(4))
    (gpre_ref, gpost_ref, win_ref, cw_ref, cb_ref, wqk_ref, wv_ref, wg_ref, bg_ref,
     lnw_ref, skip_ref, wout_ref) = (next(it) for _ in range(12))
    y_ref, convn_ref, cn_ref, nn_ref, mn_ref = (next(it) for _ in range(5))
    ubuf, q_s, k_s, v_s, h_s = (next(it) for _ in range(5))
    rdim = q_s.shape[1]
    dh = rdim // heads
    nblk, _, bs = wv_ref.shape
    lk = max(tc, LANES)

    @pl.when(pl.program_id(1) == 0)
    def _():
        if has_init:
            convn_ref[...] = conv0_ref[...]
            cn_ref[...] = c0_ref[...]
            nn_ref[...] = n0_ref[...]
            mn_ref[...] = m0_ref[...]
        else:
            convn_ref[...] = jnp.zeros(convn_ref.shape, F32)
            cn_ref[...] = jnp.zeros(cn_ref.shape, F32)
            nn_ref[...] = jnp.zeros(nn_ref.shape, F32)
            mn_ref[...] = jnp.zeros(mn_ref.shape, F32)

    x = x_ref[0]
    xn = _rmsnorm(x, gpre_ref[...]).astype(BF16)
    u = _mmw(xn, win_ref[:, :rdim])
    ubuf[CONV_BASE:CONV_BASE + tc, :] = u
    uc = _silu(_causal_conv(ubuf, convn_ref, cw_ref, cb_ref, tc))
    _conv_save_history(ubuf, convn_ref, tc)

    for n in range(nblk):
        sl = slice(n * bs, (n + 1) * bs)
        res = _mmw(uc[:, sl], wqk_ref[n])
        q_s[:, sl] = res[:, :bs]
        k_s[:, sl] = res[:, bs:]
        v_s[:, sl] = _mmw(u[:, sl], wv_ref[n])

    hr = rdim // 2
    gates = (_mmw(q_s[...], wg_ref[0:hr, :]) + _mmw(k_s[...], wg_ref[hr:2 * hr, :]) +
             _mmw(v_s[...], wg_ref[2 * hr:3 * hr, :]) + bg_ref[...])
    logf = jnp.minimum(gates, 0.0) - jnp.log1p(jnp.exp(-jnp.abs(gates)))
    causal, tri, upper = _cum_matrices(tc, lk)
    logf_p = _pad_rows(logf, lk)
    bcol = _mm_f32(tri, logf_p)
    brow = _mm_f32(_transpose_f32(logf_p), upper)
    grow = _transpose_f32(_pad_rows(gates, lk))
    lane = lax.broadcasted_iota(jnp.int32, (1, LANES), 1)
    m_row = mn_ref[0]
    m_row_new = m_row

    for h in range(heads):
        hs = slice(h * dh, (h + 1) * dh)
        fh = heads + h
        bc = bcol[:, fh:fh + 1]
        br = brow[fh:fh + 1, :]
        ir = grow[h:h + 1, :]
        ic = gates[:, h:h + 1]
        m_prev = m_row[:, h:h + 1]
        qh = q_s[:, hs]
        kh = k_s[:, hs]
        vh = v_s[:, hs]
        kh_p = _pad_rows(kh, lk)
        vh_p = _pad_rows(vh, lk)
        log_d = jnp.where(causal, bc - br + ir, NEG_INF)
        log_inter = bc + m_prev
        mt = jnp.maximum(log_inter, jnp.max(log_d, axis=-1, keepdims=True))
        s_mat = _mm_nt(qh, kh_p) * jnp.exp(log_d - mt)
        w_int = jnp.exp(log_inter - mt)
        c_h = cn_ref[0, h]
        n_h = nn_ref[0, h:h + 1, :]
        num = _mm(s_mat, vh_p) + w_int * _mm(qh, c_h)
        den = (jnp.sum(s_mat, axis=-1, keepdims=True) +
               w_int * jnp.sum(qh * n_h, axis=-1, keepdims=True))
        hc = num / jnp.maximum(jnp.abs(den), jnp.exp(-mt))
        mu = jnp.mean(hc, axis=-1, keepdims=True)
        cen = hc - mu
        var = jnp.mean(cen * cen, axis=-1, keepdims=True)
        hn = cen * lax.rsqrt(var + EPS) * lnw_ref[:, hs]
        zh = _mmw(xn, win_ref[:, rdim + h * dh:rdim + (h + 1) * dh])
        h_s[:, hs] = (hn + skip_ref[:, hs] * uc[:, hs]) * _silu(zh)
        m_new = mt[tc - 1:tc, :]
        b_last = bc[tc - 1:tc, :]
        w_state = jnp.exp(b_last - bc + ic - m_new)
        decay = jnp.exp(b_last + m_prev - m_new)
        kw = kh * w_state
        cn_ref[0, h] = decay * c_h + _mm_tn(_pad_rows(kw, lk), vh_p)
        nn_ref[0, h:h + 1, :] = decay * n_h + jnp.sum(kw, axis=0, keepdims=True)
        m_row_new = jnp.where(lane == h, m_new, m_row_new)

    mn_ref[0] = m_row_new
    y = _mmw(h_s[...], wout_ref[...])
    y_ref[0] = x + _rmsnorm(y, gpost_ref[...])


def _mlstm_layer(x, state, p, tc):
    bsz, tlen, dm = x.shape
    rdim = p["rdim"]
    heads = p["heads"]
    dh = rdim // heads
    has_init = state is not None
    grid = (bsz, tlen // tc)
    in_specs = [pl.BlockSpec((1, tc, dm), lambda b, t: (b, t, 0))]
    args = [x]
    state_specs = [pl.BlockSpec((1, CONV_K - 1, rdim), lambda b, t: (b, 0, 0)),
                   pl.BlockSpec((1, heads, dh, dh), lambda b, t: (b, 0, 0, 0)),
                   pl.BlockSpec((1, heads, dh), lambda b, t: (b, 0, 0)),
                   pl.BlockSpec((1, 1, LANES), lambda b, t: (b, 0, 0))]
    if has_init:
        conv0, c0, n0, m0 = state
        m0p = jnp.pad(m0, ((0, 0), (0, LANES - heads))).reshape(bsz, 1, LANES)
        in_specs += state_specs
        args += [conv0, c0, n0, m0p]
    weights = [p["g_pre"], p["g_post"], p["w_in"], p["conv_w"], p["conv_b"], p["w_qk"],
               p["w_v"], p["w_g"], p["b_g"], p["ln_w"], p["skip"], p["w_out"]]
    in_specs += [_vmem_spec() for _ in weights]
    args += weights
    out_shape = (jax.ShapeDtypeStruct((bsz, tlen, dm), F32),
                 jax.ShapeDtypeStruct((bsz, CONV_K - 1, rdim), F32),
                 jax.ShapeDtypeStruct((bsz, heads, dh, dh), F32),
                 jax.ShapeDtypeStruct((bsz, heads, dh), F32),
                 jax.ShapeDtypeStruct((bsz, 1, LANES), F32))
    out_specs = tuple([pl.BlockSpec((1, tc, dm), lambda b, t: (b, t, 0))] + state_specs)
    scratch = [pltpu.VMEM((CONV_BASE + tc, rdim), F32)] + [pltpu.VMEM((tc, rdim), F32)] * 4
    y, convn, cn, nn, mn = pl.pallas_call(
        functools.partial(_mlstm_kernel, tc=tc, has_init=has_init, heads=heads),
        grid=grid, in_specs=in_specs, out_specs=out_specs, out_shape=out_shape,
        scratch_shapes=scratch, name="mlstm_layer",
        compiler_params=pltpu.CompilerParams(
            dimension_semantics=("arbitrary", "arbitrary"),
            vmem_limit_bytes=VMEM_LIMIT_BYTES),
    )(*args)
    return y, (convn, cn, nn, mn[:, 0, :heads])


def _ssd_kernel(*refs, tc, has_init, groups, dstate, headdim):
    it = iter(refs)
    x_ref = next(it)
    if has_init:
        conv0_ref, s0_ref = next(it), next(it)
    (gpre_ref, gpost_ref, wzx_ref, wdt_ref, cw_ref, cb_ref, dtb_ref, alog_ref, dskip_ref,
     normw_ref, wout_ref) = (next(it) for _ in range(11))
    y_ref, convn_ref, sn_ref = next(it), next(it), next(it)
    ubuf, xbc_s, y_s = next(it), next(it), next(it)
    inner = y_s.shape[1]
    nheads = inner // headdim
    hpg = nheads // groups
    gw = inner // groups
    per_vreg = LANES // headdim
    lk = max(tc, LANES)
    cdim = xbc_s.shape[1]

    @pl.when(pl.program_id(1) == 0)
    def _():
        if has_init:
            convn_ref[...] = conv0_ref[...]
            sn_ref[...] = s0_ref[...]
        else:
            convn_ref[...] = jnp.zeros(convn_ref.shape, F32)
            sn_ref[...] = jnp.zeros(sn_ref.shape, F32)

    x = x_ref[0]
    xn = _rmsnorm(x, gpre_ref[...]).astype(BF16)
    ubuf[CONV_BASE:CONV_BASE + tc, :] = _mmw(xn, wzx_ref[:, inner:])
    xbc_s[...] = _silu(_causal_conv(ubuf, convn_ref, cw_ref, cb_ref, tc))
    _conv_save_history(ubuf, convn_ref, tc)

    dt = _softplus(_mmw(xn, wdt_ref[...]) + dtb_ref[...])
    a_neg = -jnp.exp(alog_ref[...])
    dta = dt * a_neg
    causal, tri, upper = _cum_matrices(tc, lk)
    dta_p = _pad_rows(dta, lk)
    cs = _mm_f32(tri, dta_p)
    cs_row = _mm_f32(_transpose_f32(dta_p), upper)
    ecs = jnp.exp(cs)
    cs_last = cs[tc - 1:tc, :]
    ecs_last = jnp.exp(cs_last)
    wcol = jnp.exp(cs_last - cs) * dt
    lane = lax.broadcasted_iota(jnp.int32, (1, LANES), 1)

    def spread(col, h0):
        out = col[:, h0:h0 + 1]
        for j in range(1, per_vreg):
            out = jnp.where(lane >= j * headdim, col[:, h0 + j:h0 + j + 1], out)
        return out

    for g in range(groups):
        b_g = xbc_s[:, inner + g * dstate:inner + (g + 1) * dstate]
        c_g = xbc_s[:, inner + (groups + g) * dstate:inner + (groups + g + 1) * dstate]
        b_gp = _pad_rows(b_g, lk)
        cb = _mm_nt(c_g, b_gp)
        s_g = sn_ref[0, g * hpg:(g + 1) * hpg].reshape(hpg * headdim, dstate)
        c_state = _mm_nt(c_g, s_g)
        for pr in range(hpg // per_vreg):
            h0 = g * hpg + pr * per_vreg
            ls = slice(g * gw + pr * LANES, g * gw + (pr + 1) * LANES)
            xs_t = xbc_s[:, ls]
            xdt_p = _pad_rows(xs_t * spread(dt, h0), lk)
            y_t = spread(ecs, h0) * c_state[:, pr * LANES:(pr + 1) * LANES]
            for j in range(per_vreg):
                hh = h0 + j
                dec = jnp.exp(jnp.where(causal, cs[:, hh:hh + 1] - cs_row[hh:hh + 1, :], NEG_INF))
                in_head = (lane >= j * headdim) & (lane < (j + 1) * headdim)
                y_t = y_t + _mm(cb * dec, jnp.where(in_head, xdt_p, 0.0))
            y_s[:, ls] = y_t
            upd = _mm_tn(_pad_rows(xs_t * spread(wcol, h0), lk), b_gp)
            for j in range(per_vreg):
                hh = h0 + j
                sn_ref[0, hh] = (ecs_last[:, hh:hh + 1] * sn_ref[0, hh] +
                                 upd[j * headdim:(j + 1) * headdim, :])
        gs = slice(g * gw, (g + 1) * gw)
        z_g = _mmw(xn, wzx_ref[:, gs])
        yg = (y_s[:, gs] + dskip_ref[:, gs] * xbc_s[:, gs]) * _silu(z_g)
        y_s[:, gs] = (yg * lax.rsqrt(jnp.mean(yg * yg, axis=-1, keepdims=True) + EPS)) * normw_ref[:, gs]

    out = _mmw(y_s[...], wout_ref[...])
    y_ref[0] = x + _rmsnorm(out, gpost_ref[...])


def _ssd_layer(x, state, p, tc):
    bsz, tlen, dm = x.shape
    inner = p["inner"]
    groups, dstate, headdim = p["groups"], p["dstate"], p["headdim"]
    nheads = inner // headdim
    cdim = inner + 2 * groups * dstate
    has_init = state is not None
    grid = (bsz, tlen // tc)
    in_specs = [pl.BlockSpec((1, tc, dm), lambda b, t: (b, t, 0))]
    args = [x]
    state_specs = [pl.BlockSpec((1, CONV_K - 1, cdim), lambda b, t: (b, 0, 0)),
                   pl.BlockSpec((1, nheads, headdim, dstate), lambda b, t: (b, 0, 0, 0))]
    if has_init:
        in_specs += state_specs
        args += list(state)
    weights = [p["g_pre"], p["g_post"], p["w_zx"], p["w_dt"], p["conv_w"], p["conv_b"],
               p["dt_bias"], p["a_log"], p["d_skip"], p["norm_w"], p["w_out"]]
    in_specs += [_vmem_spec() for _ in weights]
    args += weights
    out_shape = (jax.ShapeDtypeStruct((bsz, tlen, dm), F32),
                 jax.ShapeDtypeStruct((bsz, CONV_K - 1, cdim), F32),
                 jax.ShapeDtypeStruct((bsz, nheads, headdim, dstate), F32))
    out_specs = tuple([pl.BlockSpec((1, tc, dm), lambda b, t: (b, t, 0))] + state_specs)
    scratch = [pltpu.VMEM((CONV_BASE + tc, cdim), F32),
               pltpu.VMEM((tc, cdim), F32),
               pltpu.VMEM((tc, inner), F32)]
    y, convn, sn = pl.pallas_call(
        functools.partial(_ssd_kernel, tc=tc, has_init=has_init, groups=groups,
                          dstate=dstate, headdim=headdim),
        grid=grid, in_specs=in_specs, out_specs=out_specs, out_shape=out_shape,
        scratch_shapes=scratch, name="ssd_layer",
        compiler_params=pltpu.CompilerParams(
            dimension_semantics=("arbitrary", "arbitrary"),
            vmem_limit_bytes=VMEM_LIMIT_BYTES),
    )(*args)
    return y, (convn, sn)


def _row(v, width=None):
    v = v.astype(F32).reshape(1, -1)
    if width is not None and v.shape[1] < width:
        v = jnp.pad(v, ((0, 0), (0, width - v.shape[1])))
    return v


def _expand_block_diag(w, bs_out):
    nb, s, _ = w.shape
    per = bs_out // s
    w4 = w.reshape(nb // per, per, s, s)
    eye = jnp.eye(per, dtype=w.dtype)
    return jnp.einsum("nbij,bc->nbicj", w4, eye).reshape(nb // per, bs_out, bs_out)


def _pack_bf16(w):
    *lead, k, n = w.shape
    wb = w.astype(BF16).reshape(*lead, k // 2, 2, n)
    return lax.bitcast_convert_type(jnp.swapaxes(wb, -1, -2), jnp.uint32)


def _lru_params(j, g_pre, g_post, w_in, conv_w, conv_b, w_a, b_a, w_x, b_x, lam, w_out):
    return dict(g_pre=_row(g_pre), g_post=_row(g_post), w_in=_pack_bf16(w_in[j]),
                conv_w=conv_w[j].astype(F32), conv_b=_row(conv_b[j]),
                w_ax=_pack_bf16(jnp.concatenate([w_a[j], w_x[j]], axis=-1)),
                b_a=_row(b_a[j]), b_x=_row(b_x[j]), lam=_row(lam[j]),
                w_out=_pack_bf16(w_out[j]), rdim=w_out.shape[1])


def _mlstm_params(j, g_pre, g_post, w_in, conv_w, conv_b, w_q, w_k, w_v, w_ig, b_ig, w_fg,
                  b_fg, ln_w, skip, w_out):
    heads = w_ig.shape[-1]
    rdim = w_out.shape[1]
    dh = rdim // heads
    wq = _expand_block_diag(w_q[j], LANES)
    wk = _expand_block_diag(w_k[j], LANES) * (dh ** -0.5)
    wv = _expand_block_diag(w_v[j], LANES)
    w_g = jnp.concatenate([w_ig[j], w_fg[j]], axis=-1)
    w_g = jnp.pad(w_g, ((0, 0), (0, LANES - 2 * heads)))
    b_g = _row(jnp.concatenate([b_ig[j], b_fg[j]]), LANES)
    return dict(g_pre=_row(g_pre), g_post=_row(g_post), w_in=_pack_bf16(w_in[j]),
                conv_w=conv_w[j].astype(F32), conv_b=_row(conv_b[j]),
                w_qk=_pack_bf16(jnp.concatenate([wq, wk], axis=-1)), w_v=_pack_bf16(wv),
                w_g=_pack_bf16(w_g), b_g=b_g, ln_w=_row(ln_w[j]), skip=_row(skip[j]),
                w_out=_pack_bf16(w_out[j]), heads=heads, rdim=rdim)


def _ssd_params(j, g_pre, g_post, w_in, conv_w, conv_b, dt_bias, a_log, d_skip, norm_w, w_out,
                groups, dstate):
    inner = w_out.shape[1]
    nheads = dt_bias.shape[-1]
    headdim = inner // nheads
    cdim = conv_w.shape[-1]
    w = w_in[j]
    w_dt = jnp.pad(w[:, inner + cdim:], ((0, 0), (0, LANES - nheads)))
    return dict(g_pre=_row(g_pre), g_post=_row(g_post), w_zx=_pack_bf16(w[:, :inner + cdim]),
                w_dt=_pack_bf16(w_dt), conv_w=conv_w[j].astype(F32), conv_b=_row(conv_b[j]),
                dt_bias=_row(dt_bias[j], LANES), a_log=_row(a_log[j], LANES),
                d_skip=_row(jnp.repeat(d_skip[j], headdim)), norm_w=_row(norm_w[j]),
                w_out=_pack_bf16(w_out[j]), groups=groups, dstate=dstate, headdim=headdim,
                inner=inner)


SSD_GROUPS = 4
SSD_DSTATE = 128
PROMPT_CHUNK = 256


def kernel(x_prompt, x_sample, state_l0_conv, state_l0_h, state_l1_conv, state_l1_C, state_l1_n, state_l1_m, state_l2_conv, state_l2_ssm, state_l3_conv, state_l3_h, norm_pre, norm_post, lru_w_in, lru_conv_w, lru_conv_b, lru_w_a, lru_b_a, lru_w_x, lru_b_x, lru_lambda, lru_w_out, mlstm_w_in, mlstm_conv_w, mlstm_conv_b, mlstm_w_q, mlstm_w_k, mlstm_w_v, mlstm_w_ig, mlstm_b_ig, mlstm_w_fg, mlstm_b_fg, mlstm_ln_w, mlstm_skip, mlstm_w_out, ssd_w_in, ssd_conv_w, ssd_conv_b, ssd_dt_bias, ssd_a_log, ssd_d_skip, ssd_norm_w, ssd_w_out):
    lru_w = (lru_w_in, lru_conv_w, lru_conv_b, lru_w_a, lru_b_a, lru_w_x, lru_b_x, lru_lambda,
             lru_w_out)
    p0 = _lru_params(0, norm_pre[0], norm_post[0], *lru_w)
    p1 = _mlstm_params(0, norm_pre[1], norm_post[1], mlstm_w_in, mlstm_conv_w, mlstm_conv_b,
                       mlstm_w_q, mlstm_w_k, mlstm_w_v, mlstm_w_ig, mlstm_b_ig, mlstm_w_fg,
                       mlstm_b_fg, mlstm_ln_w, mlstm_skip, mlstm_w_out)
    p2 = _ssd_params(0, norm_pre[2], norm_post[2], ssd_w_in, ssd_conv_w, ssd_conv_b, ssd_dt_bias,
                     ssd_a_log, ssd_d_skip, ssd_norm_w, ssd_w_out, SSD_GROUPS, SSD_DSTATE)
    p3 = _lru_params(1, norm_pre[3], norm_post[3], *lru_w)

    def trunk(x, states, tc):
        st = states if states is not None else [None] * 4
        bsz, tlen, dm = x.shape
        seg = tc // SUBLANES
        x = x.reshape(bsz, tlen // tc, SUBLANES, seg, dm).swapaxes(2, 3).reshape(bsz, tlen, dm)
        x, s0 = _lru_layer(x, st[0], p0, tc)
        x, s1 = _mlstm_layer(x, st[1], p1, tc)
        x, s2 = _ssd_layer(x, st[2], p2, tc)
        x, s3 = _lru_layer(x, st[3], p3, tc)
        x = x.reshape(bsz, tlen // tc, seg, SUBLANES, dm).swapaxes(2, 3).reshape(bsz, tlen, dm)
        return x, (*s0, *s1, *s2, *s3)

    sample_states = [(state_l0_conv, state_l0_h),
                     (state_l1_conv, state_l1_C, state_l1_n, state_l1_m),
                     (state_l2_conv, state_l2_ssm),
                     (state_l3_conv, state_l3_h)]
    y_prompt, p_states = trunk(x_prompt, None, min(PROMPT_CHUNK, x_prompt.shape[1]))
    y_sample, s_states = trunk(x_sample, sample_states, x_sample.shape[1])
    return (y_prompt, y_sample, *p_states, *s_states)
```

```python
import functools
import math

import jax
import jax.numpy as jnp
from jax import lax
from jax.experimental import pallas as pl
from jax.experimental.pallas import tpu as pltpu

F32 = jnp.float32
BF16 = jnp.bfloat16
EPS = 1e-6
CONV_K = 4
LRU_C = 8.0
LANES = 128
SUBLANES = 8
CONV_BASE = (CONV_K - 1) * SUBLANES
VMEM_LIMIT_BYTES = 58 * 1024 * 1024
NEG_INF = float("-inf")
FAR_FUTURE = 1 << 30
TINY = 1e-30
PIECE = 256
PACK_STEPS = 32
LRU_STREAMS = 2
LOG2E = 1.4426950408889634


def _mm(a, b):
    return jnp.dot(a.astype(BF16), b.astype(BF16), preferred_element_type=F32)


def _mmw(a, w_packed):
    return jnp.dot(a.astype(BF16), pltpu.bitcast(w_packed, BF16), preferred_element_type=F32)


def _mm_nt(a, b):
    return lax.dot_general(a.astype(BF16), b.astype(BF16), (((1,), (1,)), ((), ())),
                           preferred_element_type=F32)


def _mm_tn(a, b):
    return lax.dot_general(a.astype(BF16), b.astype(BF16), (((0,), (0,)), ((), ())),
                           preferred_element_type=F32)


def _select_mm(sel, x, dims, sel_first):
    s = sel.astype(BF16)
    out = None
    rest = x
    for _ in range(3):
        term = rest.astype(BF16)
        rest = rest - term.astype(F32)
        ops = (s, term) if sel_first else (term, s)
        part = lax.dot_general(*ops, dims, preferred_element_type=F32)
        out = part if out is None else out + part
    return out


_NN = (((1,), (0,)), ((), ()))
_NT = (((1,), (1,)), ((), ()))
_TN = (((0,), (0,)), ((), ()))


def _cumsum_cols(tri, x):
    return _select_mm(tri, x, _NN, True)


def _cumsum_rows(x, upper):
    return _select_mm(upper, x, _TN, False)


def _transpose_f32(a):
    n = a.shape[1]
    eye = (lax.broadcasted_iota(jnp.int32, (n, n), 0) ==
           lax.broadcasted_iota(jnp.int32, (n, n), 1))
    return _select_mm(eye, a, _NT, True)


def _row_sum(x):
    acc = x[:, :LANES]
    for c in range(LANES, x.shape[1], LANES):
        acc = acc + x[:, c:c + LANES]
    return jnp.sum(acc, axis=-1, keepdims=True)


def _rmsnorm(x, g):
    return x * lax.rsqrt(jnp.mean(x * x, axis=-1, keepdims=True) + EPS) * g


def _sigmoid(x):
    return jax.nn.sigmoid(x)


def _silu(x):
    return x * _sigmoid(x)


def _softplus(x):
    return jnp.maximum(x, 0.0) + jnp.log1p(jnp.exp(-jnp.abs(x)))


def _pad_rows(a, rows):
    if a.shape[0] == rows:
        return a
    return jnp.concatenate([a, jnp.zeros((rows - a.shape[0],) + a.shape[1:], a.dtype)], axis=0)


def _staggered(stages, n):
    vals = [None] * n
    for step in range(n + len(stages) - 1):
        for s, stage in enumerate(stages):
            p = step - s
            if 0 <= p < n:
                vals[p] = stage(p, vals[p])
    return vals


def _row_of_time(t, tc):
    seg = tc // SUBLANES
    return (t % seg) * SUBLANES + t // seg


def _causal_conv(u, hist_ref, cw_ref, cb_ref, tc, cols):
    seg = tc // SUBLANES
    sub = lax.broadcasted_iota(jnp.int32, (SUBLANES, u.shape[1]), 0)
    lead = []
    for k in range(CONV_K - 1, 0, -1):
        shift = -(-k // seg)
        src = (shift * seg - k) * SUBLANES
        blk = pltpu.roll(u[src:src + SUBLANES, :], shift, 0)
        for s in range(shift):
            h = (CONV_K - 1) + s * seg - k
            blk = jnp.where(sub == s, hist_ref[0, h:h + 1, cols], blk)
        lead.append(blk)
    ext = jnp.concatenate(lead + [u], axis=0)
    acc = cb_ref[:, cols] + cw_ref[CONV_K - 1:CONV_K, cols] * u
    for k in range(1, CONV_K):
        off = CONV_BASE - k * SUBLANES
        acc = acc + cw_ref[CONV_K - 1 - k:CONV_K - k, cols] * ext[off:off + tc, :]
    for i in range(CONV_K - 1):
        r = _row_of_time(tc - (CONV_K - 1) + i, tc)
        hist_ref[0, i:i + 1, cols] = u[r:r + 1, :]
    return acc


def _segment_scan(a, b, h_prev, tc):
    seg = tc // SUBLANES
    width = a.shape[1]
    hs, ps = [], []
    h = prod = None
    for j in range(seg):
        a8 = a[j * SUBLANES:(j + 1) * SUBLANES, :]
        b8 = b[j * SUBLANES:(j + 1) * SUBLANES, :]
        h = b8 if h is None else a8 * h + b8
        prod = a8 if prod is None else a8 * prod
        hs.append(h)
        ps.append(prod)
    sub = lax.broadcasted_iota(jnp.int32, (SUBLANES, width), 0)
    carry_in = jnp.broadcast_to(h_prev, (SUBLANES, width))
    for s in range(1, SUBLANES):
        carry_in = jnp.where(sub == s, pltpu.roll(h + prod * carry_in, 1, 0), carry_in)
    h_last = (h + prod * carry_in)[SUBLANES - 1:SUBLANES, :]
    full = jnp.concatenate([hj + pj * carry_in for hj, pj in zip(hs, ps)], axis=0)
    return full, h_last


def _time_index(shape, dim, tc):
    r = lax.broadcasted_iota(jnp.int32, shape, dim)
    t = (r & (SUBLANES - 1)) * (tc // SUBLANES) + (r >> 3)
    if shape[dim] > tc:
        t = jnp.where(r < tc, t, FAR_FUTURE)
    return t


def _cum_matrices(lq, lk):
    causal = _time_index((lq, lk), 1, lq) <= _time_index((lq, lk), 0, lq)
    upper = (_time_index((lk, lk), 0, lq) <= _time_index((lk, lk), 1, lq)).astype(F32)
    return causal, causal.astype(F32), upper


def _lru_kernel(*refs, tc, has_init):
    it = iter(refs)
    x_ref = next(it)
    if has_init:
        conv0_ref, h0_ref = next(it), next(it)
    (gpre_ref, gpost_ref, win_ref, cw_ref, cb_ref, wax_ref, ba_ref, bx_ref, lam_ref,
     wout_ref) = (next(it) for _ in range(10))
    y_ref, convn_ref, hn_ref = next(it), next(it), next(it)
    rdim = cw_ref.shape[1]
    bs = wax_ref.shape[2] // 2

    nb = x_ref.shape[0]

    @pl.when(pl.program_id(1) == 0)
    def _():
        if has_init:
            convn_ref[...] = conv0_ref[...]
            hn_ref[...] = h0_ref[...]
        else:
            convn_ref[...] = jnp.zeros(convn_ref.shape, F32)
            hn_ref[...] = jnp.zeros(hn_ref.shape, F32)

    xs = [x_ref[s] for s in range(nb)]
    xns = [_rmsnorm(x, gpre_ref[...]).astype(BF16) for x in xs]
    rate = (-LRU_C * LOG2E) * _softplus(-lam_ref[...])
    h_prevs = [hn_ref[s] for s in range(nb)]

    def cols_of(piece):
        return slice((piece // nb) * PIECE, (piece // nb + 1) * PIECE)

    def project(piece, _):
        cols = cols_of(piece)
        xn = xns[piece % nb]
        u = _mmw(xn, win_ref[:, cols])
        g = _mmw(xn, win_ref[:, rdim + cols.start:rdim + cols.stop])
        return u, g

    def conv_and_gate_matmuls(piece, st):
        u, g = st
        cols = cols_of(piece)
        uc = _causal_conv(u, convn_ref.at[pl.ds(piece % nb, 1)], cw_ref, cb_ref, tc, cols)
        res = [_mmw(uc[:, q * bs:(q + 1) * bs], wax_ref[cols.start // bs + q])
               for q in range(PIECE // bs)]
        return uc, g, res

    def gates(piece, st):
        uc, g, res = st
        cols = cols_of(piece)
        a_parts, b_parts = [], []
        for q in range(PIECE // bs):
            sl = slice(cols.start + q * bs, cols.start + (q + 1) * bs)
            r = _sigmoid(res[q][:, :bs] + ba_ref[:, sl])
            i = _sigmoid(res[q][:, bs:] + bx_ref[:, sl])
            a = jnp.exp2(r * rate[:, sl])
            gain2 = 1.0 - a * a
            a_parts.append(a)
            b_parts.append((gain2 * lax.rsqrt(jnp.maximum(gain2, TINY))) * (i * uc[:, q * bs:(q + 1) * bs]))
        return jnp.concatenate(a_parts, axis=1), jnp.concatenate(b_parts, axis=1), _silu(g)

    def recur_and_project_out(piece, st):
        a, b, gate = st
        cols = cols_of(piece)
        s = piece % nb
        h, h_last = _segment_scan(a, b, h_prevs[s][:, cols], tc)
        hn_ref[s, :, cols] = h_last
        return _mmw(h * gate, wout_ref[cols.start // 2:cols.stop // 2, :])

    def gates_recur_and_project_out(piece, st):
        return recur_and_project_out(piece, gates(piece, st))

    parts = _staggered((project, conv_and_gate_matmuls, gates_recur_and_project_out),
                       nb * (rdim // PIECE))
    for s in range(nb):
        mine = parts[s::nb]
        y_ref[s] = xs[s] + _rmsnorm(sum(mine[1:], mine[0]), gpost_ref[...])


def _segment_prefix_max(x, tc):
    seg = tc // SUBLANES
    runs = []
    run = None
    for j in range(seg):
        blk = x[j * SUBLANES:(j + 1) * SUBLANES, :]
        run = blk if run is None else jnp.maximum(run, blk)
        runs.append(run)
    sub = lax.broadcasted_iota(jnp.int32, (SUBLANES, x.shape[1]), 0)
    before = jnp.full((SUBLANES, x.shape[1]), NEG_INF, F32)
    for s in range(1, SUBLANES):
        before = jnp.where(sub == s, pltpu.roll(jnp.maximum(before, run), 1, 0), before)
    return jnp.concatenate([jnp.maximum(r, before) for r in runs], axis=0)


def _vmem_spec():
    return pl.BlockSpec(memory_space=pltpu.VMEM)


def _lru_layer(x, state, p, tc):
    bsz, tlen, dm = x.shape
    rdim = p["rdim"]
    has_init = state is not None
    nb = math.gcd(bsz, LRU_STREAMS)
    grid = (bsz // nb, tlen // tc)
    in_specs = [pl.BlockSpec((nb, tc, dm), lambda b, t: (b, t, 0))]
    args = [x]
    if has_init:
        conv0, h0 = state
        in_specs += [pl.BlockSpec((nb, CONV_K - 1, rdim), lambda b, t: (b, 0, 0)),
                     pl.BlockSpec((nb, 1, rdim), lambda b, t: (b, 0, 0))]
        args += [conv0, h0.reshape(bsz, 1, rdim)]
    weights = [p["g_pre"], p["g_post"], p["w_in"], p["conv_w"], p["conv_b"], p["w_ax"],
               p["b_a"], p["b_x"], p["lam"], p["w_out"]]
    in_specs += [_vmem_spec() for _ in weights]
    args += weights
    out_shape = (jax.ShapeDtypeStruct((bsz, tlen, dm), F32),
                 jax.ShapeDtypeStruct((bsz, CONV_K - 1, rdim), F32),
                 jax.ShapeDtypeStruct((bsz, 1, rdim), F32))
    out_specs = (pl.BlockSpec((nb, tc, dm), lambda b, t: (b, t, 0)),
                 pl.BlockSpec((nb, CONV_K - 1, rdim), lambda b, t: (b, 0, 0)),
                 pl.BlockSpec((nb, 1, rdim), lambda b, t: (b, 0, 0)))
    y, convn, hn = pl.pallas_call(
        functools.partial(_lru_kernel, tc=tc, has_init=has_init),
        grid=grid, in_specs=in_specs, out_specs=out_specs, out_shape=out_shape,
        name="lru_layer",
        compiler_params=pltpu.CompilerParams(
            dimension_semantics=("arbitrary", "arbitrary"),
            vmem_limit_bytes=VMEM_LIMIT_BYTES),
    )(*args)
    return y, (convn, hn.reshape(bsz, rdim))


def _mlstm_kernel(*refs, tc, has_init, heads):
    it = iter(refs)
    x_ref = next(it)
    if has_init:
        conv0_ref, c0_ref, n0_ref, m0_ref = (next(it) for _ in range(4))
    (gpre_ref, gpost_ref, win_ref, cw_ref, cb_ref, wq_ref, wk_ref, wv_ref, wg_ref, bg_ref,
     lnw_ref, skip_ref, wout_ref) = (next(it) for _ in range(13))
    y_ref, convn_ref, cn_ref, nn_ref, mn_ref = (next(it) for _ in range(5))
    rdim = cw_ref.shape[1]
    dh = rdim // heads
    lk = max(tc, LANES)

    @pl.when(pl.program_id(1) == 0)
    def _():
        if has_init:
            convn_ref[...] = conv0_ref[...]
            cn_ref[...] = c0_ref[...]
            nn_ref[...] = n0_ref[...]
            mn_ref[...] = m0_ref[...]
        else:
            convn_ref[...] = jnp.zeros(convn_ref.shape, F32)
            cn_ref[...] = jnp.zeros(cn_ref.shape, F32)
            nn_ref[...] = jnp.zeros(nn_ref.shape, F32)
            mn_ref[...] = jnp.zeros(mn_ref.shape, F32)

    x = x_ref[0]
    xn = _rmsnorm(x, gpre_ref[...]).astype(BF16)
    hr = rdim // 2
    def head_cols(h):
        return slice(h * dh, (h + 1) * dh)

    def project(h, _):
        return _mmw(xn, win_ref[:, head_cols(h)])

    def conv_and_qkv(h, u):
        uc = _silu(_causal_conv(u, convn_ref, cw_ref, cb_ref, tc, head_cols(h)))
        return uc, _mmw(uc, wq_ref[h]), _mmw(uc, wk_ref[h]), _mmw(u, wv_ref[h])

    def gate_share(h, st):
        uc, q, k, v = st
        q, k, v = q.astype(BF16), k.astype(BF16), v.astype(BF16)
        ph = slice(h * dh // 2, (h + 1) * dh // 2)
        g = (_mmw(q, wg_ref[ph, :]) + _mmw(k, wg_ref[hr + ph.start:hr + ph.stop, :]) +
             _mmw(v, wg_ref[2 * hr + ph.start:2 * hr + ph.stop, :]))
        return uc, q, k, v, g

    pass1 = _staggered((project, conv_and_qkv, gate_share), heads)
    gates = bg_ref[...] + sum((st[4] for st in pass1[1:]), pass1[0][4])
    logf = jnp.minimum(gates, 0.0) - jnp.log1p(jnp.exp(-jnp.abs(gates)))
    causal, tri, upper = _cum_matrices(tc, lk)
    logf_p = _pad_rows(logf, lk)
    bcol = _cumsum_cols(tri, logf_p)
    brow = _cumsum_rows(logf_p, upper)
    grow = _transpose_f32(_pad_rows(gates, lk))
    lane = lax.broadcasted_iota(jnp.int32, (1, LANES), 1)
    m_row = mn_ref[0]
    ig_on_f = pltpu.roll(gates, heads, 1)
    mt_all = bcol + jnp.maximum(pltpu.roll(m_row, heads, 1), _segment_prefix_max(ig_on_f - bcol, tc))

    def score_matmuls(h, _):
        _, qh, kh, vh, _ = pass1[h]
        c_h = cn_ref[0, h]
        kh_p = _pad_rows(kh, lk)
        vh_p = _pad_rows(vh, lk)
        qk = _mm_nt(qh, kh_p)
        qc = _mm(qh, c_h)
        zh = _mmw(xn, win_ref[:, rdim + h * dh:rdim + (h + 1) * dh])
        return qk, qc, zh, c_h, vh_p

    def weigh(h, st):
        qk, qc, zh, c_h, vh_p = st
        fh = heads + h
        bc = bcol[:, fh:fh + 1]
        br = brow[fh:fh + 1, :]
        ir = grow[h:h + 1, :]
        m_prev = m_row[:, h:h + 1]
        mt = mt_all[:, fh:fh + 1]
        log_w = (bc - mt) + (ir - br)
        s_mat = jnp.where(causal, qk, 0.0) * jnp.exp(jnp.minimum(log_w, 0.0))
        w_int = jnp.exp(bc + m_prev - mt)
        num = _mm(s_mat, vh_p) + w_int * qc
        return s_mat, w_int, num, mt, zh, c_h, vh_p

    def finish(h, st):
        s_mat, w_int, num, mt, zh, c_h, vh_p = st
        uc, qh, kh, _, _ = pass1[h]
        hs = head_cols(h)
        fh = heads + h
        bc = bcol[:, fh:fh + 1]
        ic = gates[:, h:h + 1]
        m_prev = m_row[:, h:h + 1]
        n_h = nn_ref[0, h:h + 1, :]
        den = _row_sum(s_mat) + w_int * _row_sum(qh * n_h)
        hc = num / jnp.maximum(jnp.abs(den), jnp.exp(-mt))
        mu = _row_sum(hc) * (1.0 / dh)
        cen = hc - mu
        var = _row_sum(cen * cen) * (1.0 / dh)
        hn = cen * lax.rsqrt(var + EPS) * lnw_ref[:, hs]
        part = _mmw((hn + skip_ref[:, hs] * uc) * _silu(zh), wout_ref[hs.start // 2:hs.stop // 2, :])
        return part, mt, c_h, vh_p, n_h

    def update_state(h, st):
        part, mt, c_h, vh_p, n_h = st
        _, _, kh, _, _ = pass1[h]
        fh = heads + h
        bc = bcol[:, fh:fh + 1]
        ic = gates[:, h:h + 1]
        m_prev = m_row[:, h:h + 1]
        m_new = mt[tc - 1:tc, :]
        b_last = bc[tc - 1:tc, :]
        w_state = jnp.exp(b_last - bc + ic - m_new)
        decay = jnp.exp(b_last + m_prev - m_new)
        kw = kh * w_state
        cn_ref[0, h] = decay * c_h + _mm_tn(_pad_rows(kw, lk), vh_p)
        nn_ref[0, h:h + 1, :] = decay * n_h + jnp.sum(kw, axis=0, keepdims=True)
        return part, m_new

    def finish_and_update(h, st):
        return update_state(h, finish(h, st))

    pass2 = _staggered((score_matmuls, weigh, finish_and_update), heads)
    m_row_new = m_row
    for h in range(heads):
        m_row_new = jnp.where(lane == h, pass2[h][1], m_row_new)
    mn_ref[0] = m_row_new
    y = sum((st[0] for st in pass2[1:]), pass2[0][0])
    y_ref[0] = x + _rmsnorm(y, gpost_ref[...])


def _mlstm_layer(x, state, p, tc):
    bsz, tlen, dm = x.shape
    rdim = p["rdim"]
    heads = p["heads"]
    dh = rdim // heads
    has_init = state is not None
    grid = (bsz, tlen // tc)
    in_specs = [pl.BlockSpec((1, tc, dm), lambda b, t: (b, t, 0))]
    args = [x]
    state_specs = [pl.BlockSpec((1, CONV_K - 1, rdim), lambda b, t: (b, 0, 0)),
                   pl.BlockSpec((1, heads, dh, dh), lambda b, t: (b, 0, 0, 0)),
                   pl.BlockSpec((1, heads, dh), lambda b, t: (b, 0, 0)),
                   pl.BlockSpec((1, 1, LANES), lambda b, t: (b, 0, 0))]
    if has_init:
        conv0, c0, n0, m0 = state
        m0p = jnp.pad(m0, ((0, 0), (0, LANES - heads))).reshape(bsz, 1, LANES)
        in_specs += state_specs
        args += [conv0, c0, n0, m0p]
    weights = [p["g_pre"], p["g_post"], p["w_in"], p["conv_w"], p["conv_b"], p["w_q"], p["w_k"],
               p["w_v"], p["w_g"], p["b_g"], p["ln_w"], p["skip"], p["w_out"]]
    in_specs += [_vmem_spec() for _ in weights]
    args += weights
    out_shape = (jax.ShapeDtypeStruct((bsz, tlen, dm), F32),
                 jax.ShapeDtypeStruct((bsz, CONV_K - 1, rdim), F32),
                 jax.ShapeDtypeStruct((bsz, heads, dh, dh), F32),
                 jax.ShapeDtypeStruct((bsz, heads, dh), F32),
                 jax.ShapeDtypeStruct((bsz, 1, LANES), F32))
    out_specs = tuple([pl.BlockSpec((1, tc, dm), lambda b, t: (b, t, 0))] + state_specs)
    y, convn, cn, nn, mn = pl.pallas_call(
        functools.partial(_mlstm_kernel, tc=tc, has_init=has_init, heads=heads),
        grid=grid, in_specs=in_specs, out_specs=out_specs, out_shape=out_shape,
        name="mlstm_layer",
        compiler_params=pltpu.CompilerParams(
            dimension_semantics=("arbitrary", "arbitrary"),
            vmem_limit_bytes=VMEM_LIMIT_BYTES),
    )(*args)
    return y, (convn, cn, nn, mn[:, 0, :heads])


def _ssd_kernel(*refs, tc, has_init, groups, dstate, headdim):
    it = iter(refs)
    x_ref = next(it)
    if has_init:
        conv0_ref, s0_ref = next(it), next(it)
    (gpre_ref, gpost_ref, wzx_ref, wdt_ref, cw_ref, cb_ref, dtb_ref, alog_ref, dskip_ref,
     normw_ref, wout_ref) = (next(it) for _ in range(11))
    y_ref, convn_ref, sn_ref = next(it), next(it), next(it)
    inner = normw_ref.shape[1]
    nheads = inner // headdim
    hpg = nheads // groups
    gw = inner // groups
    per_vreg = LANES // headdim
    lk = max(tc, LANES)
    cdim = cw_ref.shape[1]

    @pl.when(pl.program_id(1) == 0)
    def _():
        if has_init:
            convn_ref[...] = conv0_ref[...]
            sn_ref[...] = s0_ref[...]
        else:
            convn_ref[...] = jnp.zeros(convn_ref.shape, F32)
            sn_ref[...] = jnp.zeros(sn_ref.shape, F32)

    x = x_ref[0]
    xn = _rmsnorm(x, gpre_ref[...]).astype(BF16)

    def project(cols):
        u = _mmw(xn, wzx_ref[:, inner + cols.start:inner + cols.stop])
        return _silu(_causal_conv(u, convn_ref, cw_ref, cb_ref, tc, cols))

    dt = _softplus(_mmw(xn, wdt_ref[...]) + dtb_ref[...])
    a_neg = -jnp.exp(alog_ref[...])
    dta = dt * a_neg
    causal, tri, upper = _cum_matrices(tc, lk)
    dta_p = _pad_rows(dta, lk)
    cs = _cumsum_cols(tri, dta_p)
    cs_row = _select_mm(upper, _transpose_f32(dta_p), _NN, False)
    ecs = jnp.exp(cs)
    cs_last = cs[tc - 1:tc, :]
    ecs_last = jnp.exp(cs_last)
    wcol = jnp.exp(cs_last - cs) * dt
    lane = lax.broadcasted_iota(jnp.int32, (1, LANES), 1)

    def spread(col, h0):
        out = col[:, h0:h0 + 1]
        for j in range(1, per_vreg):
            out = jnp.where(lane >= j * headdim, col[:, h0 + j:h0 + j + 1], out)
        return out

    bc_all = project(slice(inner, cdim))

    def group_cols(g):
        return slice(g * gw, (g + 1) * gw)

    def project_group(g, _):
        return project(group_cols(g)), _mmw(xn, wzx_ref[:, group_cols(g)])

    def group_products(g, st):
        xs_g, z_g = st
        b_g = bc_all[:, g * dstate:(g + 1) * dstate]
        c_g = bc_all[:, (groups + g) * dstate:(groups + g + 1) * dstate]
        b_gp = _pad_rows(b_g, lk).astype(BF16)
        cb = jnp.where(causal, _mm_nt(c_g, b_gp), 0.0)
        s_g = sn_ref[0, g * hpg:(g + 1) * hpg].reshape(hpg * headdim, dstate)
        c_state = _mm_nt(c_g, s_g)
        return xs_g, z_g, b_gp, cb, c_state

    def heads_of_group(g, st):
        xs_g, z_g, b_gp, cb, c_state = st

        def decayed_operands(pr, _):
            h0 = g * hpg + pr * per_vreg
            xs_t = xs_g[:, pr * LANES:(pr + 1) * LANES]
            xdt_p = _pad_rows(xs_t * spread(dt, h0), lk)
            pairs = []
            for j in range(per_vreg):
                hh = h0 + j
                dec = jnp.exp(jnp.minimum(cs[:, hh:hh + 1] - cs_row[hh:hh + 1, :], 0.0))
                in_head = (lane >= j * headdim) & (lane < (j + 1) * headdim)
                pairs.append(((cb * dec).astype(BF16), jnp.where(in_head, xdt_p, 0.0).astype(BF16)))
            xw = _pad_rows(xs_t * spread(wcol, h0), lk).astype(BF16)
            y0 = spread(ecs, h0) * c_state[:, pr * LANES:(pr + 1) * LANES]
            return pairs, xw, y0

        def products(pr, st):
            pairs, xw, y_t = st
            h0 = g * hpg + pr * per_vreg
            for m, xd in pairs:
                y_t = y_t + _mm(m, xd)
            upd = _mm_tn(xw, b_gp)
            for j in range(per_vreg):
                hh = h0 + j
                sn_ref[0, hh] = (ecs_last[:, hh:hh + 1] * sn_ref[0, hh] +
                                 upd[j * headdim:(j + 1) * headdim, :])
            return y_t

        y_tiles = _staggered((decayed_operands, products), hpg // per_vreg)
        return xs_g, z_g, jnp.concatenate(y_tiles, axis=1)

    def gate_norm_project(g, st):
        xs_g, z_g, y_g = st
        gs = group_cols(g)
        yg = (y_g + dskip_ref[:, gs] * xs_g) * _silu(z_g)
        yg = (yg * lax.rsqrt(_row_sum(yg * yg) * (1.0 / gw) + EPS)) * normw_ref[:, gs]
        return _mmw(yg, wout_ref[gs.start // 2:gs.stop // 2, :])

    parts = _staggered((project_group, group_products, heads_of_group, gate_norm_project), groups)
    y_ref[0] = x + _rmsnorm(sum(parts[1:], parts[0]), gpost_ref[...])


def _ssd_layer(x, state, p, tc):
    bsz, tlen, dm = x.shape
    inner = p["inner"]
    groups, dstate, headdim = p["groups"], p["dstate"], p["headdim"]
    nheads = inner // headdim
    cdim = inner + 2 * groups * dstate
    has_init = state is not None
    grid = (bsz, tlen // tc)
    in_specs = [pl.BlockSpec((1, tc, dm), lambda b, t: (b, t, 0))]
    args = [x]
    state_specs = [pl.BlockSpec((1, CONV_K - 1, cdim), lambda b, t: (b, 0, 0)),
                   pl.BlockSpec((1, nheads, headdim, dstate), lambda b, t: (b, 0, 0, 0))]
    if has_init:
        in_specs += state_specs
        args += list(state)
    weights = [p["g_pre"], p["g_post"], p["w_zx"], p["w_dt"], p["conv_w"], p["conv_b"],
               p["dt_bias"], p["a_log"], p["d_skip"], p["norm_w"], p["w_out"]]
    in_specs += [_vmem_spec() for _ in weights]
    args += weights
    out_shape = (jax.ShapeDtypeStruct((bsz, tlen, dm), F32),
                 jax.ShapeDtypeStruct((bsz, CONV_K - 1, cdim), F32),
                 jax.ShapeDtypeStruct((bsz, nheads, headdim, dstate), F32))
    out_specs = tuple([pl.BlockSpec((1, tc, dm), lambda b, t: (b, t, 0))] + state_specs)
    y, convn, sn = pl.pallas_call(
        functools.partial(_ssd_kernel, tc=tc, has_init=has_init, groups=groups,
                          dstate=dstate, headdim=headdim),
        grid=grid, in_specs=in_specs, out_specs=out_specs, out_shape=out_shape,
        name="ssd_layer",
        compiler_params=pltpu.CompilerParams(
            dimension_semantics=("arbitrary", "arbitrary"),
            vmem_limit_bytes=VMEM_LIMIT_BYTES),
    )(*args)
    return y, (convn, sn)


def _row(v, width=None):
    v = v.astype(F32).reshape(1, -1)
    if width is not None and v.shape[1] < width:
        v = jnp.pad(v, ((0, 0), (0, width - v.shape[1])))
    return v


def _expand_block_diag(w, bs_out):
    nb, s, _ = w.shape
    rows = w.astype(F32).reshape(nb * s // bs_out, bs_out, s)
    col = jnp.arange(bs_out)
    select = (col[None, :] % s == jnp.arange(s)[:, None]).astype(F32)
    spread = jnp.einsum("nrj,jc->nrc", rows, select, precision=lax.Precision.HIGHEST)
    return jnp.where(col[:, None] // s == col[None, :] // s, spread, 0.0)


class _PackRequest:
    def __init__(self, w, index=None, cols=None):
        self.w, self.index, self.cols = w, index, cols


def _pack_bf16(w, index=None, cols=None):
    return _PackRequest(w, index, cols)


def _pack_all(param_dicts):
    slots = [(d, key) for d in param_dicts for key, v in d.items() if isinstance(v, _PackRequest)]
    args, in_specs, out_specs, out_shapes, final_shapes = [], [], [], [], []
    for d, key in slots:
        req = d[key]
        *lead, k, n = req.w.shape
        n_out = n if req.cols is None else req.cols
        if req.index is not None:
            lead = lead[1:]
        rows = math.prod(lead) * k
        blk = rows // PACK_STEPS
        assert rows % PACK_STEPS == 0 and blk % (2 * SUBLANES) == 0, (key, rows)
        first = 0 if req.index is None else req.index * PACK_STEPS
        args.append(req.w.astype(F32).reshape(-1, n))
        in_specs.append(pl.BlockSpec((blk, n_out), functools.partial(_pack_block, first)))
        out_specs.append(pl.BlockSpec((blk // 2, n_out), functools.partial(_pack_block, 0)))
        out_shapes.append(jax.ShapeDtypeStruct((rows // 2, n_out), jnp.uint32))
        final_shapes.append((*lead, k // 2, n_out))
    packed = pl.pallas_call(
        _pack_kernel, grid=(PACK_STEPS,), in_specs=in_specs, out_specs=tuple(out_specs),
        out_shape=tuple(out_shapes), name="pack_weights",
    )(*args)
    for (d, key), out, shape in zip(slots, packed, final_shapes):
        d[key] = out.reshape(shape)


def _pack_block(first, i):
    return (i + first, 0)


def _pack_kernel(*refs):
    n = len(refs) // 2
    for w_ref, o_ref in zip(refs[:n], refs[n:]):
        o_ref[...] = pltpu.bitcast(w_ref[...].astype(BF16), jnp.uint32)


def _lru_params(j, g_pre, g_post, w_in, conv_w, conv_b, w_a, b_a, w_x, b_x, lam, w_out):
    return dict(g_pre=_row(g_pre), g_post=_row(g_post), w_in=_pack_bf16(w_in, j),
                conv_w=conv_w[j].astype(F32), conv_b=_row(conv_b[j]),
                w_ax=_pack_bf16(jnp.concatenate([w_a[j], w_x[j]], axis=-1)),
                b_a=_row(b_a[j]), b_x=_row(b_x[j]), lam=_row(lam[j]),
                w_out=_pack_bf16(w_out, j), rdim=w_out.shape[1])


def _mlstm_params(j, g_pre, g_post, w_in, conv_w, conv_b, w_q, w_k, w_v, w_ig, b_ig, w_fg,
                  b_fg, ln_w, skip, w_out):
    heads = w_ig.shape[-1]
    rdim = w_out.shape[1]
    dh = rdim // heads
    wq = _expand_block_diag(w_q[j], dh)
    wk = _expand_block_diag(w_k[j], dh) * (dh ** -0.5)
    wv = _expand_block_diag(w_v[j], dh)
    w_g = jnp.concatenate([w_ig[j], w_fg[j]], axis=-1)
    w_g = jnp.pad(w_g, ((0, 0), (0, LANES - 2 * heads)))
    b_g = _row(jnp.concatenate([b_ig[j], b_fg[j]]), LANES)
    return dict(g_pre=_row(g_pre), g_post=_row(g_post), w_in=_pack_bf16(w_in, j),
                conv_w=conv_w[j].astype(F32), conv_b=_row(conv_b[j]),
                w_q=_pack_bf16(wq), w_k=_pack_bf16(wk), w_v=_pack_bf16(wv),
                w_g=_pack_bf16(w_g), b_g=b_g, ln_w=_row(ln_w[j]), skip=_row(skip[j]),
                w_out=_pack_bf16(w_out, j), heads=heads, rdim=rdim)


def _ssd_params(j, g_pre, g_post, w_in, conv_w, conv_b, dt_bias, a_log, d_skip, norm_w, w_out,
                groups, dstate):
    inner = w_out.shape[1]
    nheads = dt_bias.shape[-1]
    headdim = inner // nheads
    cdim = conv_w.shape[-1]
    w_dt = jnp.pad(w_in[j, :, inner + cdim:], ((0, 0), (0, LANES - nheads)))
    return dict(g_pre=_row(g_pre), g_post=_row(g_post),
                w_zx=_pack_bf16(w_in, j, cols=inner + cdim),
                w_dt=_pack_bf16(w_dt), conv_w=conv_w[j].astype(F32), conv_b=_row(conv_b[j]),
                dt_bias=_row(dt_bias[j], LANES), a_log=_row(a_log[j], LANES),
                d_skip=_row(jnp.repeat(d_skip[j], headdim)), norm_w=_row(norm_w[j]),
                w_out=_pack_bf16(w_out, j), groups=groups, dstate=dstate, headdim=headdim,
                inner=inner)


SSD_GROUPS = 4
SSD_DSTATE = 128
PROMPT_CHUNK = 256


def kernel(x_prompt, x_sample, state_l0_conv, state_l0_h, state_l1_conv, state_l1_C, state_l1_n, state_l1_m, state_l2_conv, state_l2_ssm, state_l3_conv, state_l3_h, norm_pre, norm_post, lru_w_in, lru_conv_w, lru_conv_b, lru_w_a, lru_b_a, lru_w_x, lru_b_x, lru_lambda, lru_w_out, mlstm_w_in, mlstm_conv_w, mlstm_conv_b, mlstm_w_q, mlstm_w_k, mlstm_w_v, mlstm_w_ig, mlstm_b_ig, mlstm_w_fg, mlstm_b_fg, mlstm_ln_w, mlstm_skip, mlstm_w_out, ssd_w_in, ssd_conv_w, ssd_conv_b, ssd_dt_bias, ssd_a_log, ssd_d_skip, ssd_norm_w, ssd_w_out):
    lru_w = (lru_w_in, lru_conv_w, lru_conv_b, lru_w_a, lru_b_a, lru_w_x, lru_b_x, lru_lambda,
             lru_w_out)
    p0 = _lru_params(0, norm_pre[0], norm_post[0], *lru_w)
    p1 = _mlstm_params(0, norm_pre[1], norm_post[1], mlstm_w_in, mlstm_conv_w, mlstm_conv_b,
                       mlstm_w_q, mlstm_w_k, mlstm_w_v, mlstm_w_ig, mlstm_b_ig, mlstm_w_fg,
                       mlstm_b_fg, mlstm_ln_w, mlstm_skip, mlstm_w_out)
    p2 = _ssd_params(0, norm_pre[2], norm_post[2], ssd_w_in, ssd_conv_w, ssd_conv_b, ssd_dt_bias,
                     ssd_a_log, ssd_d_skip, ssd_norm_w, ssd_w_out, SSD_GROUPS, SSD_DSTATE)
    p3 = _lru_params(1, norm_pre[3], norm_post[3], *lru_w)
    _pack_all([p0, p1, p2, p3])

    def trunk(x, states, tc):
        st = states if states is not None else [None] * 4
        bsz, tlen, dm = x.shape
        seg = tc // SUBLANES
        x = x.reshape(bsz, tlen // tc, SUBLANES, seg, dm).swapaxes(2, 3).reshape(bsz, tlen, dm)
        x, s0 = _lru_layer(x, st[0], p0, tc)
        x, s1 = _mlstm_layer(x, st[1], p1, tc)
        x, s2 = _ssd_layer(x, st[2], p2, tc)
        x, s3 = _lru_layer(x, st[3], p3, tc)
        x = x.reshape(bsz, tlen // tc, seg, SUBLANES, dm).swapaxes(2, 3).reshape(bsz, tlen, dm)
        return x, (*s0, *s1, *s2, *s3)

    sample_states = [(state_l0_conv, state_l0_h),
                     (state_l1_conv, state_l1_C, state_l1_n, state_l1_m),
                     (state_l2_conv, state_l2_ssm),
                     (state_l3_conv, state_l3_h)]
    y_prompt, p_states = trunk(x_prompt, None, min(PROMPT_CHUNK, x_prompt.shape[1]))
    y_sample, s_states = trunk(x_sample, sample_states, x_sample.shape[1])
    return (y_prompt, y_sample, *p_states, *s_states)
```
